```python
import math
import jax, jax.numpy as jnp
from jax import lax
import numpy as np

D_MODEL = 1024
BATCH = 16
SEQ = 2048
DEPTH = 2

N_META = 16
MIX_WIDTH = D_MODEL
FOX_WIDTH = MIX_WIDTH // 2
RET_WIDTH = MIX_WIDTH - FOX_WIDTH
FOX_HEAD_DIM = 64
FOX_HEADS = FOX_WIDTH // FOX_HEAD_DIM
RET_HEADS = 4
RET_HEAD_DIM = RET_WIDTH // RET_HEADS
D_FF = -(-8 * D_MODEL // (3 * 256)) * 256
BLOCK_Q = 128
RET_CHUNK = 128
ROPE_BASE = 10000.0
EPS = 1e-6
FGATE_BIAS_MEAN = 3.0
RET_LOG_GAMMA = tuple(math.log(1.0 - 2.0 ** (-5 - h)) for h in range(RET_HEADS))
IN_SIZES = (FOX_WIDTH, FOX_WIDTH, FOX_WIDTH, FOX_HEADS, RET_WIDTH, RET_WIDTH, RET_WIDTH, RET_WIDTH)
IN_COLS = sum(IN_SIZES)
SPLIT_POINTS = tuple(int(v) for v in np.cumsum(IN_SIZES)[:-1])

kernel_name = "hymba_fox_retnet_hybrid"


def rmsnorm(x, g):
    x32 = x.astype(jnp.float32)
    y = x32 * lax.rsqrt(jnp.mean(x32 * x32, axis=-1, keepdims=True) + EPS)
    return (y * g.astype(jnp.float32)).astype(x.dtype)


def rotate(x, cos, sin):
    half = x.shape[-1] // 2
    x1, x2 = x[..., :half], x[..., half:]
    c = cos[None, :, None, :].astype(x.dtype)
    s = sin[None, :, None, :].astype(x.dtype)
    return jnp.concatenate([x1 * c - x2 * s, x1 * s + x2 * c], axis=-1)


def fox_attention(q, k, v, logf):
    B, L, H, Dh = q.shape
    scale = Dh ** -0.5
    c = jnp.cumsum(logf, axis=1).transpose(0, 2, 1)
    qh = q.transpose(0, 2, 1, 3)
    kh = k.transpose(0, 2, 1, 3)
    vh = v.transpose(0, 2, 1, 3)
    pos = jnp.arange(L)

    def attend(q_blk, c_q, q_pos, k_, v_, c_k, k_pos):
        s = jnp.einsum('bhqd,bhkd->bhqk', q_blk, k_).astype(jnp.float32) * scale
        s = s + c_q[..., :, None] - c_k[..., None, :]
        s = jnp.where(k_pos[None, :] <= q_pos[:, None], s, -jnp.inf)
        p = jax.nn.softmax(s, axis=-1)
        return jnp.einsum('bhqk,bhkd->bhqd', p.astype(v_.dtype), v_)

    meta_out = attend(qh[:, :, :N_META], c[:, :, :N_META], pos[:N_META],
                      kh[:, :, :N_META], vh[:, :, :N_META], c[:, :, :N_META], pos[:N_META])
    n_blk = (L - N_META) // BLOCK_Q
    q_real = qh[:, :, N_META:].reshape(B, H, n_blk, BLOCK_Q, Dh).transpose(2, 0, 1, 3, 4)
    c_real = c[:, :, N_META:].reshape(B, H, n_blk, BLOCK_Q).transpose(2, 0, 1, 3)
    pos_real = pos[N_META:].reshape(n_blk, BLOCK_Q)
    real_out = lax.map(lambda a: attend(a[0], a[1], a[2], kh, vh, c, pos),
                       (q_real, c_real, pos_real))
    real_out = real_out.transpose(1, 2, 0, 3, 4).reshape(B, H, L - N_META, Dh)
    out = jnp.concatenate([meta_out, real_out], axis=2)
    return out.transpose(0, 2, 1, 3)


def retention_chunkwise(q, k, v):
    B, L, H, Dh = q.shape
    log_g = jnp.array(RET_LOG_GAMMA, dtype=jnp.float32)

    def intra(qc, kc, vc, n):
        idx = jnp.arange(n)
        diff = idx[:, None] - idx[None, :]
        dec = jnp.where(diff >= 0,
                        jnp.exp(jnp.maximum(diff, 0)[None].astype(jnp.float32) * log_g[:, None, None]),
                        0.0)
        s = jnp.einsum('bnhd,bmhd->bhnm', qc, kc).astype(jnp.float32) * dec
        return jnp.einsum('bhnm,bmhd->bnhd', s, vc.astype(jnp.float32))

    qm, km, vm = q[:, :N_META], k[:, :N_META], v[:, :N_META]
    o_meta = intra(qm, km, vm, N_META)
    m_idx = jnp.arange(N_META, dtype=jnp.float32)
    k_dec_meta = jnp.exp((N_META - 1 - m_idx)[:, None] * log_g[None, :])
    state0 = jnp.einsum('bmhd,bmhe->bhde', km.astype(jnp.float32) * k_dec_meta[None, :, :, None],
                        vm.astype(jnp.float32))

    C = RET_CHUNK
    n_chunks = (L - N_META) // C
    j_idx = jnp.arange(C, dtype=jnp.float32)
    q_dec = jnp.exp((j_idx + 1.0)[:, None] * log_g[None, :])[None, :, :, None]
    k_dec = jnp.exp((C - 1.0 - j_idx)[:, None] * log_g[None, :])[None, :, :, None]
    chunk_dec = jnp.exp(C * log_g)[None, :, None, None]

    def to_chunks(t):
        return t[:, N_META:].reshape(B, n_chunks, C, H, Dh).transpose(1, 0, 2, 3, 4)

    def step(state, xs):
        qc, kc, vc = xs
        o = intra(qc, kc, vc, C) + jnp.einsum('bjhd,bhde->bjhe', qc.astype(jnp.float32) * q_dec, state)
        state = chunk_dec * state + jnp.einsum('bjhd,bjhe->bhde', kc.astype(jnp.float32) * k_dec,
                                               vc.astype(jnp.float32))
        return state, o

    _, o_real = lax.scan(step, state0, (to_chunks(q), to_chunks(k), to_chunks(v)))
    o_real = o_real.transpose(1, 0, 2, 3, 4).reshape(B, L - N_META, H, Dh)
    return jnp.concatenate([o_meta, o_real], axis=1)


def head_groupnorm(o, g):
    mu = jnp.mean(o, axis=-1, keepdims=True)
    var = jnp.mean(jnp.square(o - mu), axis=-1, keepdims=True)
    y = (o - mu) * lax.rsqrt(var + EPS)
    B, L, H, Dh = o.shape
    return y.reshape(B, L, H * Dh) * g.astype(jnp.float32)


def hybrid_layer(h, attn_norm_g, w_in, b_fgate, ret_gn_g, w_out,
                 ffn_norm_g, w_gate, w_up, w_down, cos, sin):
    B, L, _ = h.shape
    xn = rmsnorm(h, attn_norm_g)
    proj = jnp.einsum('bld,dc->blc', xn, w_in)
    fq, fk, fv, flog, rq, rk, rv, rg = jnp.split(proj, SPLIT_POINTS, axis=-1)

    logf = jax.nn.log_sigmoid(flog.astype(jnp.float32) + b_fgate.astype(jnp.float32))
    fshape = (B, L, FOX_HEADS, FOX_HEAD_DIM)
    fox_out = fox_attention(fq.reshape(fshape), fk.reshape(fshape), fv.reshape(fshape), logf)
    fox_out = fox_out.reshape(B, L, FOX_WIDTH)

    rshape = (B, L, RET_HEADS, RET_HEAD_DIM)
    rq_ = rotate(rq.reshape(rshape), cos, sin)
    rk_ = rotate(rk.reshape(rshape), cos, sin) * (RET_HEAD_DIM ** -0.5)
    ret = retention_chunkwise(rq_, rk_, rv.reshape(rshape))
    ret = head_groupnorm(ret, ret_gn_g).astype(h.dtype)
    ret_out = jax.nn.silu(rg) * ret

    mix = jnp.concatenate([fox_out, ret_out], axis=-1)
    h = h + jnp.einsum('blc,cd->bld', mix, w_out)

    hn = rmsnorm(h, ffn_norm_g)
    ff = jax.nn.silu(jnp.einsum('bld,df->blf', hn, w_gate)) * jnp.einsum('bld,df->blf', hn, w_up)
    return h + jnp.einsum('blf,fd->bld', ff, w_down)


def setup_inputs(seed: int = 0) -> dict:
    key = jax.random.key(seed)
    ks = jax.random.split(key, 12)
    f32 = jnp.float32
    x = jax.random.normal(ks[0], (BATCH, SEQ, D_MODEL), f32)
    meta_tokens = jax.random.normal(ks[1], (N_META, D_MODEL), f32)
    attn_norm = 1.0 + 0.02 * jax.random.normal(ks[2], (DEPTH, D_MODEL), f32)
    w_in = jax.random.normal(ks[3], (DEPTH, D_MODEL, IN_COLS), f32) * D_MODEL ** -0.5
    b_fgate = FGATE_BIAS_MEAN + 0.5 * jax.random.normal(ks[4], (DEPTH, FOX_HEADS), f32)
    ret_gn = 1.0 + 0.02 * jax.random.normal(ks[5], (DEPTH, RET_WIDTH), f32)
    w_out = jax.random.normal(ks[6], (DEPTH, MIX_WIDTH, D_MODEL), f32) * MIX_WIDTH ** -0.5
    ffn_norm = 1.0 + 0.02 * jax.random.normal(ks[7], (DEPTH, D_MODEL), f32)
    w_gate = jax.random.normal(ks[8], (DEPTH, D_MODEL, D_FF), f32) * D_MODEL ** -0.5
    w_up = jax.random.normal(ks[9], (DEPTH, D_MODEL, D_FF), f32) * D_MODEL ** -0.5
    w_down = jax.random.normal(ks[10], (DEPTH, D_FF, D_MODEL), f32) * D_FF ** -0.5
    final_norm = 1.0 + 0.02 * jax.random.normal(ks[11], (D_MODEL,), f32)
    return {"x": x, "meta_tokens": meta_tokens, "attn_norm": attn_norm, "w_in": w_in,
            "b_fgate": b_fgate, "ret_gn": ret_gn, "w_out": w_out, "ffn_norm": ffn_norm,
            "w_gate": w_gate, "w_up": w_up, "w_down": w_down, "final_norm": final_norm}


def reference(x, meta_tokens, attn_norm, w_in, b_fgate, ret_gn, w_out, ffn_norm,
              w_gate, w_up, w_down, final_norm):
    B = x.shape[0]
    meta = jnp.broadcast_to(meta_tokens[None].astype(x.dtype), (B, N_META, D_MODEL))
    h = jnp.concatenate([meta, x], axis=1)
    L = h.shape[1]
    inv_freq = ROPE_BASE ** (-jnp.arange(0, RET_HEAD_DIM, 2, dtype=jnp.float32) / RET_HEAD_DIM)
    ang = jnp.arange(L, dtype=jnp.float32)[:, None] * inv_freq[None, :]
    cos, sin = jnp.cos(ang), jnp.sin(ang)
    for i in range(DEPTH):
        h = hybrid_layer(h, attn_norm[i], w_in[i], b_fgate[i], ret_gn[i], w_out[i],
                         ffn_norm[i], w_gate[i], w_up[i], w_down[i], cos, sin)
    h = rmsnorm(h, final_norm)
    return h[:, N_META:]
```

```python
import functools
import math

import jax
import jax.numpy as jnp
from jax import lax
from jax.experimental import pallas as pl
from jax.experimental.pallas import tpu as pltpu

N_META = 16
FOX_HEAD_DIM = 64
RET_HEADS = 4
RET_HEAD_DIM = 128
RET_CHUNK = 128
ROPE_BASE = 10000.0
EPS = 1e-6
RET_LOG_GAMMA = tuple(math.log(1.0 - 2.0 ** (-5 - h)) for h in range(RET_HEADS))

LANES = 128
BF16_SUBLANES = 16
VMEM_LIMIT_BYTES = 56 * 1024 * 1024

F32 = jnp.float32
BF16 = jnp.bfloat16


def _dot(a, b):
    return jnp.dot(a, b, preferred_element_type=F32)


def _dot_nt(a, b):
    return lax.dot_general(a, b, (((1,), (1,)), ((), ())), preferred_element_type=F32)


def _dot_tn(a, b):
    return lax.dot_general(a, b, (((0,), (0,)), ((), ())), preferred_element_type=F32)


def _rmsnorm(x, g):
    return x * lax.rsqrt(jnp.mean(x * x, axis=-1, keepdims=True) + EPS) * g


def _row_tile(n, cap):
    best = None
    for t in range(BF16_SUBLANES, min(n, cap) + 1, BF16_SUBLANES):
        if n % t == 0:
            best = t
    assert best is not None, n
    return best


def _const_spec(shape):
    nd = len(shape)
    return pl.BlockSpec(shape, lambda *_: (0,) * nd, pipeline_mode=pl.Buffered(1))


def _inproj_kernel(h_ref, g_ref, wf_ref, wr_ref, wl_ref, bl_ref,
                   fox_ref, ret_ref, logf_ref, *, fox_w, col_chunk):
    xn = _rmsnorm(h_ref[...], g_ref[...]).astype(BF16)
    q_scale = FOX_HEAD_DIM ** -0.5
    for c0 in range(0, 3 * fox_w, col_chunk):
        y = _dot(xn, wf_ref[:, c0:c0 + col_chunk])
        if c0 < fox_w:
            y = y * q_scale
        fox_ref[:, c0:c0 + col_chunk] = y.astype(BF16)
    for c0 in range(0, wr_ref.shape[1], col_chunk):
        ret_ref[:, c0:c0 + col_chunk] = _dot(xn, wr_ref[:, c0:c0 + col_chunk]).astype(BF16)
    z = _dot(xn, wl_ref[...]) + bl_ref[...]
    logf_ref[...] = jnp.minimum(z, 0.0) - jnp.log(1.0 + jnp.exp(-jnp.abs(z)))


def _inproj(h, g, wf, wr, wl, bl, *, tm):
    n, d = h.shape
    fox_cols, ret_cols = wf.shape[1], wr.shape[1]
    kern = functools.partial(_inproj_kernel, fox_w=fox_cols // 3, col_chunk=512)
    return pl.pallas_call(
        kern,
        grid=(n // tm,),
        in_specs=[
            pl.BlockSpec((tm, d), lambda i: (i, 0)),
            _const_spec((1, d)),
            _const_spec(wf.shape),
            _const_spec(wr.shape),
            _const_spec(wl.shape),
            _const_spec((1, LANES)),
        ],
        out_specs=[
            pl.BlockSpec((tm, fox_cols), lambda i: (i, 0)),
            pl.BlockSpec((tm, ret_cols), lambda i: (i, 0)),
            pl.BlockSpec((tm, LANES), lambda i: (i, 0)),
        ],
        out_shape=[
            jax.ShapeDtypeStruct((n, fox_cols), BF16),
            jax.ShapeDtypeStruct((n, ret_cols), BF16),
            jax.ShapeDtypeStruct((n, LANES), F32),
        ],
        compiler_params=pltpu.CompilerParams(
            dimension_semantics=("arbitrary",), vmem_limit_bytes=VMEM_LIMIT_BYTES),
        name="inproj",
    )(h, g, wf, wr, wl, bl)


def _split3(x):
    hi = x.astype(BF16)
    r1 = x - hi.astype(F32)
    mid = r1.astype(BF16)
    lo = (r1 - mid.astype(F32)).astype(BF16)
    return hi, mid, lo


def _decay_kernel(logf_ref, ccol_ref, crow_ref, crowm_ref, *, n_blk, n_heads):
    blk = LANES
    ri = lax.broadcasted_iota(jnp.int32, (blk, blk), 0)
    ci = lax.broadcasted_iota(jnp.int32, (blk, blk), 1)
    tri = (ci <= ri).astype(BF16)

    def prefix(x):
        hi, mid, lo = _split3(x)
        return _dot(tri, hi) + _dot(tri, mid) + _dot(tri, lo)

    xm = jnp.concatenate([logf_ref[0:N_META, :], jnp.zeros((blk - N_META, LANES), F32)], axis=0)
    cm = prefix(xm)
    ccol_ref[0:N_META, :] = cm[0:N_META, :]
    cm_t = cm.T
    for hh in range(n_heads):
        crowm_ref[hh] = cm_t[hh:hh + 1, :]

    def body(b, carry):
        s = pl.multiple_of(N_META + b * blk, BF16_SUBLANES)
        cb = prefix(logf_ref[pl.ds(s, blk), :]) + carry
        ccol_ref[pl.ds(s, blk), :] = cb
        cb_t = cb.T
        for hh in range(n_heads):
            crow_ref[hh, :, pl.ds(pl.multiple_of(b * blk, blk), blk)] = cb_t[hh:hh + 1, :]
        return cb[blk - 1:blk, :]

    lax.fori_loop(0, n_blk, body, cm[N_META - 1:N_META, :])


def _decay(logf, *, batch, n_heads):
    n = logf.shape[0]
    seq_l = n // batch
    seq = seq_l - N_META
    logf3 = logf.reshape(batch, seq_l, LANES)
    kern = functools.partial(_decay_kernel, n_blk=seq // LANES, n_heads=n_heads)
    return pl.pallas_call(
        kern,
        grid=(batch,),
        in_specs=[pl.BlockSpec((None, seq_l, LANES), lambda b: (b, 0, 0))],
        out_specs=[
            pl.BlockSpec((None, seq_l, LANES), lambda b: (b, 0, 0)),
            pl.BlockSpec((None, n_heads, 1, seq), lambda b: (b, 0, 0, 0)),
            pl.BlockSpec((None, n_heads, 1, LANES), lambda b: (b, 0, 0, 0)),
        ],
        out_shape=[
            jax.ShapeDtypeStruct((batch, seq_l, LANES), F32),
            jax.ShapeDtypeStruct((batch, n_heads, 1, seq), F32),
            jax.ShapeDtypeStruct((batch, n_heads, 1, LANES), F32),
        ],
        compiler_params=pltpu.CompilerParams(
            dimension_semantics=("arbitrary",), vmem_limit_bytes=VMEM_LIMIT_BYTES),
        name="decay",
    )(logf3)


def _fox_kernel(q_ref, k_ref, v_ref, ccol_ref, crow_ref, crowm_ref, o_ref,
                kmp_scr, vmp_scr, s_scr, m_scr, l_scr, acc_scr, *, seq, tile):
    hp = pl.program_id(1)
    n_qblk = seq // tile
    neg_inf = -jnp.inf
    lane = lax.broadcasted_iota(jnp.int32, (1, LANES), 1)
    head_lanes = [(lane >= FOX_HEAD_DIM * a) & (lane < FOX_HEAD_DIM * (a + 1)) for a in (0, 1)]

    pad = jnp.zeros((LANES - N_META, LANES), BF16)
    kmp_scr[...] = jnp.concatenate([k_ref[0:N_META, :], pad], axis=0)
    vmp_scr[...] = jnp.concatenate([v_ref[0:N_META, :], pad], axis=0)

    def c_query(start, rows, a):
        ct = ccol_ref[pl.ds(start, rows), :]
        return jnp.sum(jnp.where(lane == 2 * hp + a, ct, 0.0), axis=1, keepdims=True)

    qm = q_ref[0:N_META, :]
    row_m = lax.broadcasted_iota(jnp.int32, (N_META, LANES), 0)
    col_m = lax.broadcasted_iota(jnp.int32, (N_META, LANES), 1)
    outs = []
    for a in (0, 1):
        qh = jnp.where(head_lanes[a], qm, jnp.zeros_like(qm))
        s = _dot_nt(qh, kmp_scr[...]) + (c_query(0, N_META, a) - crowm_ref[a])
        s = jnp.where(col_m <= row_m, s, neg_inf)
        m = jnp.max(s, axis=1, keepdims=True)
        p = jnp.exp(s - m)
        l = jnp.sum(p, axis=1, keepdims=True)
        outs.append(_dot(p.astype(BF16), vmp_scr[...]) / l)
    o_ref[0:N_META, :] = jnp.where(head_lanes[0], outs[0], outs[1]).astype(o_ref.dtype)

    row_t = lax.broadcasted_iota(jnp.int32, (tile, tile), 0)
    col_t = lax.broadcasted_iota(jnp.int32, (tile, tile), 1)

    def fold(x):
        parts = [x[:, c:c + LANES] for c in range(0, tile, LANES)]
        return parts

    def qblock(i, _):
        qs = pl.multiple_of(N_META + i * tile, BF16_SUBLANES)
        q = q_ref[pl.ds(qs, tile), :]
        outs = []
        for a in (0, 1):
            qh = jnp.where(head_lanes[a], q, jnp.zeros_like(q))
            cq = c_query(qs, tile, a)
            sm = _dot_nt(qh, kmp_scr[...]) + jnp.where(lane < N_META, cq - crowm_ref[a], neg_inf)
            s_scr[:, 0:LANES] = sm
            m_scr[...] = sm

            def scores(j):
                ks = pl.multiple_of(N_META + j * tile, BF16_SUBLANES)
                ck = crow_ref[a, :, pl.ds(pl.multiple_of(j * tile, tile), tile)]
                return _dot_nt(qh, k_ref[pl.ds(ks, tile), :]) + (cq - ck)

            def put(j, s):
                s_scr[:, pl.ds(pl.multiple_of(LANES + j * tile, LANES), tile)] = s
                mx = m_scr[...]
                for part in fold(s):
                    mx = jnp.maximum(mx, part)
                m_scr[...] = mx

            def pass1(j, _):
                put(j, scores(j))
                return 0

            lax.fori_loop(0, i, pass1, 0)
            put(i, jnp.where(col_t <= row_t, scores(i), neg_inf))
            m = jnp.max(m_scr[...], axis=1, keepdims=True)

            pm = jnp.exp(s_scr[:, 0:LANES] - m)
            l_scr[...] = pm
            acc_scr[...] = _dot(pm.astype(BF16), vmp_scr[...])

            def pass2(j, _):
                ks = pl.multiple_of(N_META + j * tile, BF16_SUBLANES)
                p = jnp.exp(s_scr[:, pl.ds(pl.multiple_of(LANES + j * tile, LANES), tile)] - m)
                ls = l_scr[...]
                for part in fold(p):
                    ls = ls + part
                l_scr[...] = ls
                acc_scr[...] += _dot(p.astype(BF16), v_ref[pl.ds(ks, tile), :])
                return 0

            lax.fori_loop(0, i + 1, pass2, 0)
            l = jnp.sum(l_scr[...], axis=1, keepdims=True)
            outs.append(acc_scr[...] / l)
        o_ref[pl.ds(qs, tile), :] = jnp.where(head_lanes[0], outs[0], outs[1]).astype(o_ref.dtype)
        return 0

    lax.fori_loop(0, n_qblk, qblock, 0)


def _fox(fox_qkv, ccol, crow, crowm, *, batch, tile):
    n, cols = fox_qkv.shape
    seq_l = n // batch
    seq = seq_l - N_META
    width = cols // 3
    n_pairs = width // LANES
    qkv3 = fox_qkv.reshape(batch, seq_l, cols)
    kern = functools.partial(_fox_kernel, seq=seq, tile=tile)
    return pl.pallas_call(
        kern,
        grid=(batch, n_pairs),
        in_specs=[
            pl.BlockSpec((None, seq_l, LANES), lambda b, p: (b, 0, p)),
            pl.BlockSpec((None, seq_l, LANES), lambda b, p: (b, 0, n_pairs + p)),
            pl.BlockSpec((None, seq_l, LANES), lambda b, p: (b, 0, 2 * n_pairs + p)),
            pl.BlockSpec((None, seq_l, LANES), lambda b, p: (b, 0, 0)),
            pl.BlockSpec((None, 2, 1, seq), lambda b, p: (b, p, 0, 0)),
            pl.BlockSpec((None, 2, 1, LANES), lambda b, p: (b, p, 0, 0)),
        ],
        out_specs=pl.BlockSpec((None, seq_l, LANES), lambda b, p: (b, 0, p)),
        out_shape=jax.ShapeDtypeStruct((batch, seq_l, width), BF16),
        scratch_shapes=[
            pltpu.VMEM((LANES, LANES), BF16),
            pltpu.VMEM((LANES, LANES), BF16),
            pltpu.VMEM((tile, LANES + seq), F32),
            pltpu.VMEM((tile, LANES), F32),
            pltpu.VMEM((tile, LANES), F32),
            pltpu.VMEM((tile, LANES), F32),
        ],
        compiler_params=pltpu.CompilerParams(
            dimension_semantics=("arbitrary", "arbitrary"), vmem_limit_bytes=VMEM_LIMIT_BYTES),
        name="fox",
    )(qkv3, qkv3, qkv3, ccol, crow, crowm)


def _ret_kernel(q_ref, k_ref, v_ref, g_ref, cs_ref, sn_ref, gn_ref, o_ref, state_scr,
                *, n_chunks):
    C = RET_CHUNK
    ri = lax.broadcasted_iota(jnp.int32, (C, C), 0)
    ci = lax.broadcasted_iota(jnp.int32, (C, C), 1)
    diff = ri - ci
    diff_f = jnp.maximum(diff, 0).astype(F32)
    j_idx = lax.broadcasted_iota(jnp.int32, (C, 1), 0).astype(F32)
    k_scale = RET_HEAD_DIM ** -0.5

    def rot(x, cs, sn):
        return x * cs + pltpu.roll(x, RET_HEAD_DIM // 2, axis=1) * sn

    def head_inputs(rows, hh, pad_rows):
        sl = slice(hh * RET_HEAD_DIM, (hh + 1) * RET_HEAD_DIM)
        q = q_ref[rows, sl].astype(F32)
        k = k_ref[rows, sl].astype(F32)
        v = v_ref[rows, sl]
        cs = cs_ref[rows, :]
        sn = sn_ref[rows, :]
        q = rot(q, cs, sn)
        k = rot(k, cs, sn) * k_scale
        if pad_rows:
            q = jnp.concatenate([q, jnp.zeros((pad_rows, RET_HEAD_DIM), F32)], axis=0)
            k = jnp.concatenate([k, jnp.zeros((pad_rows, RET_HEAD_DIM), F32)], axis=0)
            v = jnp.concatenate([v, jnp.zeros((pad_rows, RET_HEAD_DIM), v.dtype)], axis=0)
        return q, k, v

    def finish(o, rows, hh, n_rows):
        sl = slice(hh * RET_HEAD_DIM, (hh + 1) * RET_HEAD_DIM)
        o = o[0:n_rows, :]
        mu = jnp.mean(o, axis=-1, keepdims=True)
        d = o - mu
        var = jnp.mean(d * d, axis=-1, keepdims=True)
        y = d * lax.rsqrt(var + EPS) * gn_ref[:, sl]
        g = g_ref[rows, sl].astype(F32)
        o_ref[rows, sl] = (g * (1.0 / (1.0 + jnp.exp(-g))) * y).astype(o_ref.dtype)

    def intra(q, k, v, log_g):
        dec = jnp.where(diff >= 0, jnp.exp(diff_f * log_g), 0.0)
        s = _dot_nt(q.astype(BF16), k.astype(BF16)) * dec
        return _dot(s.astype(BF16), v)

    rows_m = pl.ds(0, N_META)
    for hh in range(RET_HEADS):
        log_g = RET_LOG_GAMMA[hh]
        q, k, v = head_inputs(rows_m, hh, C - N_META)
        finish(intra(q, k, v, log_g), rows_m, hh, N_META)
        k_dec = jnp.exp((N_META - 1.0 - j_idx) * log_g)
        state_scr[hh] = _dot_tn((k * k_dec).astype(BF16), v)

    def chunk(c, _):
        rows = pl.ds(pl.multiple_of(N_META + c * C, BF16_SUBLANES), C)
        for hh in range(RET_HEADS):
            log_g = RET_LOG_GAMMA[hh]
            q, k, v = head_inputs(rows, hh, 0)
            q_dec = jnp.exp((j_idx + 1.0) * log_g)
            k_dec = jnp.exp((C - 1.0 - j_idx) * log_g)
            chunk_dec = jnp.exp(jnp.full((1, 1), C * log_g, F32))
            state = state_scr[hh]
            o = intra(q, k, v, log_g) + _dot((q * q_dec).astype(BF16), state.astype(BF16))
            state_scr[hh] = chunk_dec * state + _dot_tn((k * k_dec).astype(BF16), v)
            finish(o, rows, hh, C)
        return 0

    lax.fori_loop(0, n_chunks, chunk, 0)


def _retention(ret_qkvg, cs, sn, gn, *, batch):
    n, cols = ret_qkvg.shape
    seq_l = n // batch
    width = cols // 4
    x3 = ret_qkvg.reshape(batch, seq_l, cols)
    kern = functools.partial(_ret_kernel, n_chunks=(seq_l - N_META) // RET_CHUNK)

    def part(j):
        return pl.BlockSpec((None, seq_l, width), lambda b: (b, 0, j))

    return pl.pallas_call(
        kern,
        grid=(batch,),
        in_specs=[part(0), part(1), part(2), part(3),
                  _const_spec(cs.shape), _const_spec(sn.shape), _const_spec(gn.shape)],
        out_specs=pl.BlockSpec((None, seq_l, width), lambda b: (b, 0, 0)),
        out_shape=jax.ShapeDtypeStruct((batch, seq_l, width), BF16),
        scratch_shapes=[pltpu.VMEM((RET_HEADS, RET_HEAD_DIM, RET_HEAD_DIM), F32)],
        compiler_params=pltpu.CompilerParams(
            dimension_semantics=("arbitrary",), vmem_limit_bytes=VMEM_LIMIT_BYTES),
        name="retention",
    )(x3, x3, x3, x3, cs, sn, gn)


def _outffn_kernel(h_ref, fox_ref, ret_ref, wof_ref, wor_ref, g2_ref, wg_ref, wu_ref, wd_ref,
                   gf_ref, o_ref, hn_scr, acc_scr, *, n_ff_chunks, final):
    h1 = h_ref[...] + _dot(fox_ref[...], wof_ref[...]) + _dot(ret_ref[...], wor_ref[...])
    hn_scr[...] = _rmsnorm(h1, g2_ref[...]).astype(BF16)
    acc_scr[...] = h1

    def ff(c, _):
        hn = hn_scr[...]
        gate = _dot(hn, wg_ref[c])
        up = _dot(hn, wu_ref[c])
        act = (gate * (1.0 / (1.0 + jnp.exp(-gate))) * up).astype(BF16)
        acc_scr[...] += _dot(act, wd_ref[c])
        return 0

    lax.fori_loop(0, n_ff_chunks, ff, 0)
    out = acc_scr[...]
    if final:
        out = _rmsnorm(out, gf_ref[...])
    o_ref[...] = out


def _outffn(h, fox_out, ret_out, wof, wor, g2, wg, wu, wd, gf, *, tm, final):
    n, d = h.shape
    kern = functools.partial(_outffn_kernel, n_ff_chunks=wg.shape[0], final=final)
    return pl.pallas_call(
        kern,
        grid=(n // tm,),
        in_specs=[
            pl.BlockSpec((tm, d), lambda i: (i, 0)),
            pl.BlockSpec((tm, fox_out.shape[1]), lambda i: (i, 0)),
            pl.BlockSpec((tm, ret_out.shape[1]), lambda i: (i, 0)),
            _const_spec(wof.shape), _const_spec(wor.shape), _const_spec(g2.shape),
            _const_spec(wg.shape), _const_spec(wu.shape), _const_spec(wd.shape),
            _const_spec(gf.shape),
        ],
        out_specs=pl.BlockSpec((tm, d), lambda i: (i, 0)),
        out_shape=jax.ShapeDtypeStruct((n, d), F32),
        scratch_shapes=[pltpu.VMEM((tm, d), BF16), pltpu.VMEM((tm, d), F32)],
        compiler_params=pltpu.CompilerParams(
            dimension_semantics=("arbitrary",), vmem_limit_bytes=VMEM_LIMIT_BYTES),
        name="outffn",
    )(h, fox_out, ret_out, wof, wor, g2, wg, wu, wd, gf)


FF_CHUNK = 256


def kernel(x, meta_tokens, attn_norm, w_in, b_fgate, ret_gn, w_out, ffn_norm,
           w_gate, w_up, w_down, final_norm):
    batch, seq, d = x.shape
    depth = w_in.shape[0]
    fox_heads = b_fgate.shape[1]
    fox_w = fox_heads * FOX_HEAD_DIM
    ret_w = RET_HEADS * RET_HEAD_DIM
    d_ff = w_gate.shape[2]
    seq_l = seq + N_META
    n = batch * seq_l
    assert w_in.shape[2] == 3 * fox_w + fox_heads + 4 * ret_w
    assert fox_w % LANES == 0 and fox_heads <= LANES and d_ff % FF_CHUNK == 0
    attn_tile = 256 if seq % 256 == 0 else LANES
    assert seq % attn_tile == 0 and seq % RET_CHUNK == 0
    tm = _row_tile(n, 768)

    meta = jnp.broadcast_to(meta_tokens[None].astype(x.dtype), (batch, N_META, d))
    h = jnp.concatenate([meta, x], axis=1).reshape(n, d)

    inv_freq = ROPE_BASE ** (-jnp.arange(0, RET_HEAD_DIM, 2, dtype=F32) / RET_HEAD_DIM)
    ang = jnp.arange(seq_l, dtype=F32)[:, None] * inv_freq[None, :]
    cos, sin = jnp.cos(ang), jnp.sin(ang)
    cs = jnp.concatenate([cos, cos], axis=1)
    sn = jnp.concatenate([-sin, sin], axis=1)

    n_ffc = d_ff // FF_CHUNK
    gf = final_norm.reshape(1, d)
    for i in range(depth):
        w = w_in[i]
        wf = w[:, :3 * fox_w].astype(BF16)
        wl = jnp.pad(w[:, 3 * fox_w:3 * fox_w + fox_heads], ((0, 0), (0, LANES - fox_heads))).astype(BF16)
        wr = w[:, 3 * fox_w + fox_heads:].astype(BF16)
        bl = jnp.pad(b_fgate[i], (0, LANES - fox_heads)).reshape(1, LANES)
        fox_qkv, ret_qkvg, logf = _inproj(h, attn_norm[i].reshape(1, d), wf, wr, wl, bl, tm=tm)

        ccol, crow, crowm = _decay(logf, batch=batch, n_heads=fox_heads)
        fox_out = _fox(fox_qkv, ccol, crow, crowm, batch=batch, tile=attn_tile)
        ret_out = _retention(ret_qkvg, cs, sn, ret_gn[i].reshape(1, ret_w), batch=batch)

        wo = w_out[i].astype(BF16)
        wg = w_gate[i].astype(BF16).reshape(d, n_ffc, FF_CHUNK).transpose(1, 0, 2)
        wu = w_up[i].astype(BF16).reshape(d, n_ffc, FF_CHUNK).transpose(1, 0, 2)
        wd = w_down[i].astype(BF16).reshape(n_ffc, FF_CHUNK, d)
        h = _outffn(h, fox_out.reshape(n, fox_w), ret_out.reshape(n, ret_w),
                    wo[:fox_w], wo[fox_w:], ffn_norm[i].reshape(1, d), wg, wu, wd, gf,
                    tm=tm, final=(i == depth - 1))
    return h.reshape(batch, seq_l, d)[:, N_META:]
```

```python
import functools
import math

import numpy as np
import jax
import jax.numpy as jnp
from jax import lax
from jax.experimental import pallas as pl
from jax.experimental.pallas import tpu as pltpu

N_META = 16
FOX_HEAD_DIM = 64
RET_HEADS = 4
RET_HEAD_DIM = 128
RET_CHUNK = 128
ROPE_BASE = 10000.0
EPS = 1e-6
RET_LOG_GAMMA = tuple(math.log(1.0 - 2.0 ** (-5 - h)) for h in range(RET_HEADS))
LOG2E = math.log2(math.e)

LANES = 128
BF16_SUBLANES = 16
VMEM_LIMIT_BYTES = 56 * 1024 * 1024

N_SPLIT = 3
MASKED_SCORE = -1e30

F32 = jnp.float32
BF16 = jnp.bfloat16


def _dot(a, b):
    return jnp.dot(a, b, preferred_element_type=F32)


def _dot_nt(a, b):
    return lax.dot_general(a, b, (((1,), (1,)), ((), ())), preferred_element_type=F32)


def _dot_tn(a, b):
    return lax.dot_general(a, b, (((0,), (0,)), ((), ())), preferred_element_type=F32)


def _rmsnorm(x, g):
    return x * lax.rsqrt(jnp.mean(x * x, axis=-1, keepdims=True) + EPS) * g


def _row_tile(n, cap):
    best = None
    for t in range(BF16_SUBLANES, min(n, cap) + 1, BF16_SUBLANES):
        if n % t == 0:
            best = t
    assert best is not None, n
    return best


def _const_spec(shape):
    nd = len(shape)
    return pl.BlockSpec(shape, lambda *_: (0,) * nd, pipeline_mode=pl.Buffered(1))


def _inproj_kernel(h_ref, g_ref, wf_ref, wr_ref, wl_ref, bl_ref,
                   fox_ref, ret_ref, logf_ref, *, fox_w, col_chunk):
    xn = _rmsnorm(h_ref[...], g_ref[...]).astype(BF16)
    q_scale = FOX_HEAD_DIM ** -0.5 * LOG2E
    for c0 in range(0, 3 * fox_w, col_chunk):
        y = _dot(xn, wf_ref[:, c0:c0 + col_chunk])
        if c0 < fox_w:
            y = y * q_scale
        fox_ref[:, c0:c0 + col_chunk] = y.astype(BF16)
    for c0 in range(0, wr_ref.shape[1], col_chunk):
        ret_ref[:, c0:c0 + col_chunk] = _dot(xn, wr_ref[:, c0:c0 + col_chunk]).astype(BF16)
    z = _dot(xn, wl_ref[...]) + bl_ref[...]
    logf_ref[...] = jnp.minimum(z, 0.0) - jnp.log(1.0 + jnp.exp(-jnp.abs(z)))


def _inproj(h, g, wf, wr, wl, bl, *, tm):
    n, d = h.shape
    fox_cols, ret_cols = wf.shape[1], wr.shape[1]
    kern = functools.partial(_inproj_kernel, fox_w=fox_cols // 3, col_chunk=512)
    return pl.pallas_call(
        kern,
        grid=(n // tm,),
        in_specs=[
            pl.BlockSpec((tm, d), lambda i: (i, 0)),
            _const_spec((1, d)),
            _const_spec(wf.shape),
            _const_spec(wr.shape),
            _const_spec(wl.shape),
            _const_spec((1, LANES)),
        ],
        out_specs=[
            pl.BlockSpec((tm, fox_cols), lambda i: (i, 0)),
            pl.BlockSpec((tm, ret_cols), lambda i: (i, 0)),
            pl.BlockSpec((tm, LANES), lambda i: (i, 0)),
        ],
        out_shape=[
            jax.ShapeDtypeStruct((n, fox_cols), BF16),
            jax.ShapeDtypeStruct((n, ret_cols), BF16),
            jax.ShapeDtypeStruct((n, LANES), F32),
        ],
        compiler_params=pltpu.CompilerParams(
            dimension_semantics=("arbitrary",), vmem_limit_bytes=VMEM_LIMIT_BYTES),
        name="inproj",
    )(h, g, wf, wr, wl, bl)


def _split3(x):
    hi = x.astype(BF16)
    r1 = x - hi.astype(F32)
    mid = r1.astype(BF16)
    lo = (r1 - mid.astype(F32)).astype(BF16)
    return hi, mid, lo


def _bias_selector(n_pairs):
    sel = np.zeros((N_SPLIT * LANES, n_pairs * LANES), np.float32)
    for hp in range(n_pairs):
        for a, base in ((1, 0), (0, FOX_HEAD_DIM)):
            head = 2 * hp + a
            for p in range(N_SPLIT):
                sel[p * LANES + head, hp * LANES + base + p] = 1.0
                sel[p * LANES + head, hp * LANES + base + N_SPLIT + p] = -1.0
    return sel


def _decay_kernel(logf_ref, sel_ref, qaug_ref, kaug_ref, c_scr, *, n_blk):
    blk = LANES
    ri = lax.broadcasted_iota(jnp.int32, (blk, blk), 0)
    ci = lax.broadcasted_iota(jnp.int32, (blk, blk), 1)
    tri = (ci <= ri).astype(BF16)

    def prefix(x):
        hi, mid, lo = _split3(x)
        return _dot(tri, hi) + _dot(tri, mid) + _dot(tri, lo)

    xm = jnp.concatenate([logf_ref[0:N_META, :], jnp.zeros((blk - N_META, LANES), F32)], axis=0)
    cm = prefix(xm)
    c_scr[0:N_META, :] = cm[0:N_META, :]

    def body(b, carry):
        s = pl.multiple_of(N_META + b * blk, BF16_SUBLANES)
        cb = prefix(logf_ref[pl.ds(s, blk), :]) + carry
        c_scr[pl.ds(s, blk), :] = cb
        return cb[blk - 1:blk, :]

    lax.fori_loop(0, n_blk, body, cm[N_META - 1:N_META, :])

    hi, mid, lo = _split3(c_scr[...] * LOG2E)
    routed = _dot(jnp.concatenate([hi, mid, lo], axis=1), sel_ref[...])
    lane = lax.broadcasted_iota(jnp.int32, (1, routed.shape[1]), 1) % FOX_HEAD_DIM
    first = lane < N_SPLIT
    second = (lane >= N_SPLIT) & (lane < 2 * N_SPLIT)
    qaug_ref[...] = jnp.where(first, routed, jnp.where(second, 1.0, 0.0)).astype(BF16)
    kaug_ref[...] = jnp.where(second, routed, jnp.where(first, 1.0, 0.0)).astype(BF16)


def _decay(logf, sel, *, batch):
    n = logf.shape[0]
    seq_l = n // batch
    seq = seq_l - N_META
    width = sel.shape[1]
    logf3 = logf.reshape(batch, seq_l, LANES)
    kern = functools.partial(_decay_kernel, n_blk=seq // LANES)
    out = pl.BlockSpec((None, seq_l, width), lambda b: (b, 0, 0))
    return pl.pallas_call(
        kern,
        grid=(batch,),
        in_specs=[pl.BlockSpec((None, seq_l, LANES), lambda b: (b, 0, 0)), _const_spec(sel.shape)],
        out_specs=[out, out],
        out_shape=[jax.ShapeDtypeStruct((batch, seq_l, width), BF16)] * 2,
        scratch_shapes=[pltpu.VMEM((seq_l, LANES), F32)],
        compiler_params=pltpu.CompilerParams(
            dimension_semantics=("arbitrary",), vmem_limit_bytes=VMEM_LIMIT_BYTES),
        name="decay",
    )(logf3, sel)


def _fox_kernel(q_ref, k_ref, v_ref, qaug_ref, kaug_ref, o_ref,
                k_scr, v_scr, s_scr, p_scr, *, seq, tile):
    seq_l = seq + N_META
    n_qblk = seq // tile
    pad_rows = LANES - N_META
    lane = lax.broadcasted_iota(jnp.int32, (1, LANES), 1)
    own = [lane < FOX_HEAD_DIM, lane >= FOX_HEAD_DIM]
    free0 = [FOX_HEAD_DIM, 0]

    for a in (0, 1):
        ones_col = jnp.where(lane == free0[a], 1.0, 0.0).astype(BF16)
        pad_key = jnp.where(lane == free0[a] + N_SPLIT, MASKED_SCORE, 0.0).astype(BF16)
        for dst, src in ((slice(0, N_META), slice(0, N_META)), (slice(LANES, LANES + seq), slice(N_META, seq_l))):
            k_scr[a, dst, :] = jnp.where(own[a], k_ref[src, :], kaug_ref[src, :])
            v_scr[a, dst, :] = jnp.where(own[a], v_ref[src, :], ones_col)
        k_scr[a, N_META:LANES, :] = jnp.broadcast_to(pad_key, (pad_rows, LANES))
        v_scr[a, N_META:LANES, :] = jnp.zeros((pad_rows, LANES), BF16)

    def attend(q_rows, n_rows, key_tiles, slot):
        q = q_ref[q_rows, :]
        qa = qaug_ref[q_rows, :]
        kt = key_tiles[-1][0] + key_tiles[-1][1]
        outs = []
        for a in (0, 1):
            qh = jnp.where(own[a], q, qa)
            mx = None
            for c0, w, causal in key_tiles:
                s = _dot_nt(qh, k_scr[a, c0:c0 + w, :])
                if causal:
                    row = lax.broadcasted_iota(jnp.int32, (n_rows, w), 0)
                    col = lax.broadcasted_iota(jnp.int32, (n_rows, w), 1)
                    s = jnp.where(col <= row, s, -jnp.inf)
                s_scr[slot, 0:n_rows, c0:c0 + w] = s
                for c in range(0, w, LANES):
                    part = s[:, c:c + LANES]
                    mx = part if mx is None else jnp.maximum(mx, part)
            m = jnp.broadcast_to(jnp.max(mx, axis=1, keepdims=True), (n_rows, LANES))
            for c in range(0, kt, LANES):
                p_scr[slot, 0:n_rows, c:c + LANES] = jnp.exp2(
                    s_scr[slot, 0:n_rows, c:c + LANES] - m).astype(BF16)
            acc = _dot(p_scr[slot, 0:n_rows, 0:kt], v_scr[a, 0:kt, :])
            l = acc[:, free0[a]:free0[a] + 1]
            outs.append(acc * (1.0 / l))
            slot = 1 - slot
        o_ref[q_rows, :] = jnp.where(own[0], outs[0], outs[1]).astype(o_ref.dtype)

    attend(slice(0, N_META), N_META, [(0, LANES, True)], 0)
    for i in range(n_qblk):
        tiles = [(0, LANES, False)]
        tiles += [(LANES + j * tile, tile, j == i) for j in range(i + 1)]
        attend(slice(N_META + i * tile, N_META + (i + 1) * tile), tile, tiles, 0)


def _fox(fox_qkv, qaug, kaug, *, batch, tile):
    n, cols = fox_qkv.shape
    seq_l = n // batch
    seq = seq_l - N_META
    width = cols // 3
    n_pairs = width // LANES
    qkv3 = fox_qkv.reshape(batch, seq_l, cols)
    kern = functools.partial(_fox_kernel, seq=seq, tile=tile)

    def col_block(j0):
        return pl.BlockSpec((None, seq_l, LANES), lambda b, p: (b, 0, j0 + p))

    return pl.pallas_call(
        kern,
        grid=(batch, n_pairs),
        in_specs=[col_block(0), col_block(n_pairs), col_block(2 * n_pairs), col_block(0), col_block(0)],
        out_specs=col_block(0),
        out_shape=jax.ShapeDtypeStruct((batch, seq_l, width), BF16),
        scratch_shapes=[
            pltpu.VMEM((2, LANES + seq, LANES), BF16),
            pltpu.VMEM((2, LANES + seq, LANES), BF16),
            pltpu.VMEM((2, tile, LANES + seq), F32),
            pltpu.VMEM((2, tile, LANES + seq), BF16),
        ],
        compiler_params=pltpu.CompilerParams(
            dimension_semantics=("arbitrary", "arbitrary"), vmem_limit_bytes=VMEM_LIMIT_BYTES),
        name="fox",
    )(qkv3, qkv3, qkv3, qaug, kaug)


def _ret_kernel(q_ref, k_ref, v_ref, g_ref, cs_ref, sn_ref, gn_ref, o_ref, state_scr,
                *, n_chunks):
    C = RET_CHUNK
    ri = lax.broadcasted_iota(jnp.int32, (C, C), 0)
    ci = lax.broadcasted_iota(jnp.int32, (C, C), 1)
    diff = ri - ci
    diff_f = jnp.maximum(diff, 0).astype(F32)
    j_idx = lax.broadcasted_iota(jnp.int32, (C, 1), 0).astype(F32)
    k_scale = RET_HEAD_DIM ** -0.5

    def rot(x, cs, sn):
        return x * cs + pltpu.roll(x, RET_HEAD_DIM // 2, axis=1) * sn

    def head_inputs(rows, hh, pad_rows):
        sl = slice(hh * RET_HEAD_DIM, (hh + 1) * RET_HEAD_DIM)
        q = q_ref[rows, sl].astype(F32)
        k = k_ref[rows, sl].astype(F32)
        v = v_ref[rows, sl]
        cs = cs_ref[rows, :]
        sn = sn_ref[rows, :]
        q = rot(q, cs, sn)
        k = rot(k, cs, sn) * k_scale
        if pad_rows:
            q = jnp.concatenate([q, jnp.zeros((pad_rows, RET_HEAD_DIM), F32)], axis=0)
            k = jnp.concatenate([k, jnp.zeros((pad_rows, RET_HEAD_DIM), F32)], axis=0)
            v = jnp.concatenate([v, jnp.zeros((pad_rows, RET_HEAD_DIM), v.dtype)], axis=0)
        return q, k, v

    def finish(o, rows, hh, n_rows):
        sl = slice(hh * RET_HEAD_DIM, (hh + 1) * RET_HEAD_DIM)
        o = o[0:n_rows, :]
        mu = jnp.mean(o, axis=-1, keepdims=True)
        d = o - mu
        var = jnp.mean(d * d, axis=-1, keepdims=True)
        y = d * lax.rsqrt(var + EPS) * gn_ref[:, sl]
        g = g_ref[rows, sl].astype(F32)
        o_ref[rows, sl] = (g * (1.0 / (1.0 + jnp.exp(-g))) * y).astype(o_ref.dtype)

    def intra(q, k, v, log_g):
        dec = jnp.where(diff >= 0, jnp.exp(diff_f * log_g), 0.0)
        s = _dot_nt(q.astype(BF16), k.astype(BF16)) * dec
        return _dot(s.astype(BF16), v)

    rows_m = pl.ds(0, N_META)
    for hh in range(RET_HEADS):
        log_g = RET_LOG_GAMMA[hh]
        q, k, v = head_inputs(rows_m, hh, C - N_META)
        finish(intra(q, k, v, log_g), rows_m, hh, N_META)
        k_dec = jnp.exp((N_META - 1.0 - j_idx) * log_g)
        state_scr[hh] = _dot_tn((k * k_dec).astype(BF16), v)

    def chunk(c, _):
        rows = pl.ds(pl.multiple_of(N_META + c * C, BF16_SUBLANES), C)
        for hh in range(RET_HEADS):
            log_g = RET_LOG_GAMMA[hh]
            q, k, v = head_inputs(rows, hh, 0)
            q_dec = jnp.exp((j_idx + 1.0) * log_g)
            k_dec = jnp.exp((C - 1.0 - j_idx) * log_g)
            chunk_dec = jnp.exp(jnp.full((1, 1), C * log_g, F32))
            state = state_scr[hh]
            o = intra(q, k, v, log_g) + _dot((q * q_dec).astype(BF16), state.astype(BF16))
            state_scr[hh] = chunk_dec * state + _dot_tn((k * k_dec).astype(BF16), v)
            finish(o, rows, hh, C)
        return 0

    lax.fori_loop(0, n_chunks, chunk, 0)


def _retention(ret_qkvg, cs, sn, gn, *, batch):
    n, cols = ret_qkvg.shape
    seq_l = n // batch
    width = cols // 4
    x3 = ret_qkvg.reshape(batch, seq_l, cols)
    kern = functools.partial(_ret_kernel, n_chunks=(seq_l - N_META) // RET_CHUNK)

    def part(j):
        return pl.BlockSpec((None, seq_l, width), lambda b: (b, 0, j))

    return pl.pallas_call(
        kern,
        grid=(batch,),
        in_specs=[part(0), part(1), part(2), part(3),
                  _const_spec(cs.shape), _const_spec(sn.shape), _const_spec(gn.shape)],
        out_specs=pl.BlockSpec((None, seq_l, width), lambda b: (b, 0, 0)),
        out_shape=jax.ShapeDtypeStruct((batch, seq_l, width), BF16),
        scratch_shapes=[pltpu.VMEM((RET_HEADS, RET_HEAD_DIM, RET_HEAD_DIM), F32)],
        compiler_params=pltpu.CompilerParams(
            dimension_semantics=("arbitrary",), vmem_limit_bytes=VMEM_LIMIT_BYTES),
        name="retention",
    )(x3, x3, x3, x3, cs, sn, gn)


def _outffn_kernel(h_ref, fox_ref, ret_ref, wof_ref, wor_ref, g2_ref, wg_ref, wu_ref, wd_ref,
                   gf_ref, o_ref, hn_scr, acc_scr, *, n_ff_chunks, final):
    h1 = h_ref[...] + _dot(fox_ref[...], wof_ref[...]) + _dot(ret_ref[...], wor_ref[...])
    hn_scr[...] = _rmsnorm(h1, g2_ref[...]).astype(BF16)
    acc_scr[...] = h1

    def ff(c, _):
        hn = hn_scr[...]
        gate = _dot(hn, wg_ref[c])
        up = _dot(hn, wu_ref[c])
        act = (gate * (1.0 / (1.0 + jnp.exp(-gate))) * up).astype(BF16)
        acc_scr[...] += _dot(act, wd_ref[c])
        return 0

    lax.fori_loop(0, n_ff_chunks, ff, 0)
    out = acc_scr[...]
    if final:
        out = _rmsnorm(out, gf_ref[...])
    o_ref[...] = out


def _outffn(h, fox_out, ret_out, wof, wor, g2, wg, wu, wd, gf, *, tm, final):
    n, d = h.shape
    kern = functools.partial(_outffn_kernel, n_ff_chunks=wg.shape[0], final=final)
    return pl.pallas_call(
        kern,
        grid=(n // tm,),
        in_specs=[
            pl.BlockSpec((tm, d), lambda i: (i, 0)),
            pl.BlockSpec((tm, fox_out.shape[1]), lambda i: (i, 0)),
            pl.BlockSpec((tm, ret_out.shape[1]), lambda i: (i, 0)),
            _const_spec(wof.shape), _const_spec(wor.shape), _const_spec(g2.shape),
            _const_spec(wg.shape), _const_spec(wu.shape), _const_spec(wd.shape),
            _const_spec(gf.shape),
        ],
        out_specs=pl.BlockSpec((tm, d), lambda i: (i, 0)),
        out_shape=jax.ShapeDtypeStruct((n, d), F32),
        scratch_shapes=[pltpu.VMEM((tm, d), BF16), pltpu.VMEM((tm, d), F32)],
        compiler_params=pltpu.CompilerParams(
            dimension_semantics=("arbitrary",), vmem_limit_bytes=VMEM_LIMIT_BYTES),
        name="outffn",
    )(h, fox_out, ret_out, wof, wor, g2, wg, wu, wd, gf)


FF_CHUNK = 256


def kernel(x, meta_tokens, attn_norm, w_in, b_fgate, ret_gn, w_out, ffn_norm,
           w_gate, w_up, w_down, final_norm):
    batch, seq, d = x.shape
    depth = w_in.shape[0]
    fox_heads = b_fgate.shape[1]
    fox_w = fox_heads * FOX_HEAD_DIM
    ret_w = RET_HEADS * RET_HEAD_DIM
    d_ff = w_gate.shape[2]
    seq_l = seq + N_META
    n = batch * seq_l
    assert w_in.shape[2] == 3 * fox_w + fox_heads + 4 * ret_w
    assert fox_w % LANES == 0 and fox_heads <= LANES and d_ff % FF_CHUNK == 0
    attn_tile = 256 if seq % 256 == 0 else LANES
    assert seq % attn_tile == 0 and seq % RET_CHUNK == 0
    tm = _row_tile(n, 768)

    meta = jnp.broadcast_to(meta_tokens[None].astype(x.dtype), (batch, N_META, d))
    h = jnp.concatenate([meta, x], axis=1).reshape(n, d)

    inv_freq = ROPE_BASE ** (-jnp.arange(0, RET_HEAD_DIM, 2, dtype=F32) / RET_HEAD_DIM)
    ang = jnp.arange(seq_l, dtype=F32)[:, None] * inv_freq[None, :]
    cos, sin = jnp.cos(ang), jnp.sin(ang)
    cs = jnp.concatenate([cos, cos], axis=1)
    sn = jnp.concatenate([-sin, sin], axis=1)

    sel = jnp.asarray(_bias_selector(fox_w // LANES), BF16)
    n_ffc = d_ff // FF_CHUNK
    gf = final_norm.reshape(1, d)
    for i in range(depth):
        w = w_in[i]
        wf = w[:, :3 * fox_w].astype(BF16)
        wl = jnp.pad(w[:, 3 * fox_w:3 * fox_w + fox_heads], ((0, 0), (0, LANES - fox_heads))).astype(BF16)
        wr = w[:, 3 * fox_w + fox_heads:].astype(BF16)
        bl = jnp.pad(b_fgate[i], (0, LANES - fox_heads)).reshape(1, LANES)
        fox_qkv, ret_qkvg, logf = _inproj(h, attn_norm[i].reshape(1, d), wf, wr, wl, bl, tm=tm)

        qaug, kaug = _decay(logf, sel, batch=batch)
        fox_out = _fox(fox_qkv, qaug, kaug, batch=batch, tile=attn_tile)
        ret_out = _retention(ret_qkvg, cs, sn, ret_gn[i].reshape(1, ret_w), batch=batch)

        wo = w_out[i].astype(BF16)
        wg = w_gate[i].astype(BF16).reshape(d, n_ffc, FF_CHUNK).transpose(1, 0, 2)
        wu = w_up[i].astype(BF16).reshape(d, n_ffc, FF_CHUNK).transpose(1, 0, 2)
        wd = w_down[i].astype(BF16).reshape(n_ffc, FF_CHUNK, d)
        h = _outffn(h, fox_out.reshape(n, fox_w), ret_out.reshape(n, ret_w),
                    wo[:fox_w], wo[fox_w:], ffn_norm[i].reshape(1, d), wg, wu, wd, gf,
                    tm=tm, final=(i == depth - 1))
    return h.reshape(batch, seq_l, d)[:, N_META:]
```

```python
import functools
import math

import numpy as np
import jax
import jax.numpy as jnp
from jax import lax
from jax.experimental import pallas as pl
from jax.experimental.pallas import tpu as pltpu

N_META = 16
FOX_HEAD_DIM = 64
RET_HEADS = 4
RET_HEAD_DIM = 128
RET_CHUNK = 128
ROPE_BASE = 10000.0
EPS = 1e-6
RET_LOG_GAMMA = tuple(math.log(1.0 - 2.0 ** (-5 - h)) for h in range(RET_HEADS))
LOG2E = math.log2(math.e)

LANES = 128
BF16_SUBLANES = 16
VMEM_LIMIT_BYTES = 56 * 1024 * 1024

N_SPLIT = 3
MASKED_SCORE = -1e30

F32 = jnp.float32
BF16 = jnp.bfloat16


def _dot(a, b):
    return jnp.dot(a, b, preferred_element_type=F32)


def _dot_nt(a, b):
    return lax.dot_general(a, b, (((1,), (1,)), ((), ())), preferred_element_type=F32)


def _dot_tn(a, b):
    return lax.dot_general(a, b, (((0,), (0,)), ((), ())), preferred_element_type=F32)


def _rmsnorm(x, g):
    return x * lax.rsqrt(jnp.mean(x * x, axis=-1, keepdims=True) + EPS) * g


def _row_tile(n, cap):
    best = None
    for t in range(BF16_SUBLANES, min(n, cap) + 1, BF16_SUBLANES):
        if n % t == 0:
            best = t
    assert best is not None, n
    return best


def _const_spec(shape):
    nd = len(shape)
    return pl.BlockSpec(shape, lambda *_: (0,) * nd, pipeline_mode=pl.Buffered(1))


def _inproj_kernel(h_ref, g_ref, wf_ref, wr_ref, wl_ref, bl_ref,
                   fox_ref, ret_ref, logf_ref, *, fox_w, col_chunk):
    xn = _rmsnorm(h_ref[...], g_ref[...]).astype(BF16)
    q_scale = FOX_HEAD_DIM ** -0.5 * LOG2E
    for c0 in range(0, 3 * fox_w, col_chunk):
        y = _dot(xn, wf_ref[:, c0:c0 + col_chunk])
        if c0 < fox_w:
            y = y * q_scale
        fox_ref[:, c0:c0 + col_chunk] = y.astype(BF16)
    for c0 in range(0, wr_ref.shape[1], col_chunk):
        ret_ref[:, c0:c0 + col_chunk] = _dot(xn, wr_ref[:, c0:c0 + col_chunk]).astype(BF16)
    z = _dot(xn, wl_ref[...]) + bl_ref[...]
    logf_ref[...] = jnp.minimum(z, 0.0) - jnp.log(1.0 + jnp.exp(-jnp.abs(z)))


def _inproj(h, g, wf, wr, wl, bl, *, tm):
    n, d = h.shape
    fox_cols, ret_cols = wf.shape[1], wr.shape[1]
    kern = functools.partial(_inproj_kernel, fox_w=fox_cols // 3, col_chunk=512)
    return pl.pallas_call(
        kern,
        grid=(n // tm,),
        in_specs=[
            pl.BlockSpec((tm, d), lambda i: (i, 0)),
            _const_spec((1, d)),
            _const_spec(wf.shape),
            _const_spec(wr.shape),
            _const_spec(wl.shape),
            _const_spec((1, LANES)),
        ],
        out_specs=[
            pl.BlockSpec((tm, fox_cols), lambda i: (i, 0)),
            pl.BlockSpec((tm, ret_cols), lambda i: (i, 0)),
            pl.BlockSpec((tm, LANES), lambda i: (i, 0)),
        ],
        out_shape=[
            jax.ShapeDtypeStruct((n, fox_cols), BF16),
            jax.ShapeDtypeStruct((n, ret_cols), BF16),
            jax.ShapeDtypeStruct((n, LANES), F32),
        ],
        compiler_params=pltpu.CompilerParams(
            dimension_semantics=("arbitrary",), vmem_limit_bytes=VMEM_LIMIT_BYTES),
        name="inproj",
    )(h, g, wf, wr, wl, bl)


def _split3(x):
    hi = x.astype(BF16)
    r1 = x - hi.astype(F32)
    mid = r1.astype(BF16)
    lo = (r1 - mid.astype(F32)).astype(BF16)
    return hi, mid, lo


def _bias_selector(n_pairs):
    sel = np.zeros((N_SPLIT * LANES, n_pairs * LANES), np.float32)
    for hp in range(n_pairs):
        for a, base in ((1, 0), (0, FOX_HEAD_DIM)):
            head = 2 * hp + a
            for p in range(N_SPLIT):
                sel[p * LANES + head, hp * LANES + base + p] = 1.0
                sel[p * LANES + head, hp * LANES + base + N_SPLIT + p] = -1.0
    return sel


def _decay_kernel(logf_ref, sel_ref, qaug_ref, kaug_ref, c_scr):
    blk = LANES
    ri = lax.broadcasted_iota(jnp.int32, (blk, blk), 0)
    ci = lax.broadcasted_iota(jnp.int32, (blk, blk), 1)
    tri = (ci <= ri).astype(BF16)

    def prefix(x):
        hi, mid, lo = _split3(x)
        return _dot(tri, hi) + _dot(tri, mid) + _dot(tri, lo)

    seq_l = c_scr.shape[0]
    carry = None
    for r0 in range(0, seq_l, blk):
        rows = min(blk, seq_l - r0)
        x = logf_ref[r0:r0 + rows, :]
        if rows < blk:
            x = jnp.concatenate([x, jnp.zeros((blk - rows, LANES), F32)], axis=0)
        cb = prefix(x)
        if carry is not None:
            cb = cb + carry
        c_scr[r0:r0 + rows, :] = cb[0:rows, :]
        carry = cb[blk - 1:blk, :]

    hi, mid, lo = _split3(c_scr[...] * LOG2E)
    routed = _dot(jnp.concatenate([hi, mid, lo], axis=1), sel_ref[...])
    lane = lax.broadcasted_iota(jnp.int32, (1, routed.shape[1]), 1) % FOX_HEAD_DIM
    first = lane < N_SPLIT
    second = (lane >= N_SPLIT) & (lane < 2 * N_SPLIT)
    qaug_ref[...] = jnp.where(first, routed, jnp.where(second, 1.0, 0.0)).astype(BF16)
    kaug_ref[...] = jnp.where(second, routed, jnp.where(first, 1.0, 0.0)).astype(BF16)


def _decay(logf, sel, *, batch):
    n = logf.shape[0]
    seq_l = n // batch
    width = sel.shape[1]
    logf3 = logf.reshape(batch, seq_l, LANES)
    out = pl.BlockSpec((None, seq_l, width), lambda b: (b, 0, 0))
    return pl.pallas_call(
        _decay_kernel,
        grid=(batch,),
        in_specs=[pl.BlockSpec((None, seq_l, LANES), lambda b: (b, 0, 0)), _const_spec(sel.shape)],
        out_specs=[out, out],
        out_shape=[jax.ShapeDtypeStruct((batch, seq_l, width), BF16)] * 2,
        scratch_shapes=[pltpu.VMEM((seq_l, LANES), F32)],
        compiler_params=pltpu.CompilerParams(
            dimension_semantics=("arbitrary",), vmem_limit_bytes=VMEM_LIMIT_BYTES),
        name="decay",
    )(logf3, sel)


def _fox_kernel(q_ref, k_ref, v_ref, qaug_ref, kaug_ref, o_ref,
                k_scr, v_scr, s_scr, p_scr, *, seq, tile):
    seq_l = seq + N_META
    n_qblk = seq // tile
    pad_rows = LANES - N_META
    lane = lax.broadcasted_iota(jnp.int32, (1, LANES), 1)
    own = [lane < FOX_HEAD_DIM, lane >= FOX_HEAD_DIM]
    free0 = [FOX_HEAD_DIM, 0]

    for a in (0, 1):
        ones_col = jnp.where(lane == free0[a], 1.0, 0.0).astype(BF16)
        pad_key = jnp.where(lane == free0[a] + N_SPLIT, MASKED_SCORE, 0.0).astype(BF16)
        for dst, src in ((slice(0, N_META), slice(0, N_META)), (slice(LANES, LANES + seq), slice(N_META, seq_l))):
            k_scr[a, dst, :] = jnp.where(own[a], k_ref[src, :], kaug_ref[src, :])
            v_scr[a, dst, :] = jnp.where(own[a], v_ref[src, :], ones_col)
        k_scr[a, N_META:LANES, :] = jnp.broadcast_to(pad_key, (pad_rows, LANES))
        v_scr[a, N_META:LANES, :] = jnp.zeros((pad_rows, LANES), BF16)

    def attend(q_rows, n_rows, key_tiles, slot):
        q = q_ref[q_rows, :]
        qa = qaug_ref[q_rows, :]
        kt = key_tiles[-1][0] + key_tiles[-1][1]
        outs = []
        for a in (0, 1):
            qh = jnp.where(own[a], q, qa)
            mx = None
            for c0, w, causal in key_tiles:
                s = _dot_nt(qh, k_scr[a, c0:c0 + w, :])
                if causal:
                    row = lax.broadcasted_iota(jnp.int32, (n_rows, w), 0)
                    col = lax.broadcasted_iota(jnp.int32, (n_rows, w), 1)
                    s = jnp.where(col <= row, s, -jnp.inf)
                s_scr[slot, 0:n_rows, c0:c0 + w] = s
                for c in range(0, w, LANES):
                    part = s[:, c:c + LANES]
                    mx = part if mx is None else jnp.maximum(mx, part)
            m = jnp.broadcast_to(jnp.max(mx, axis=1, keepdims=True), (n_rows, LANES))
            for c in range(0, kt, LANES):
                p_scr[slot, 0:n_rows, c:c + LANES] = jnp.exp2(
                    s_scr[slot, 0:n_rows, c:c + LANES] - m).astype(BF16)
            acc = _dot(p_scr[slot, 0:n_rows, 0:kt], v_scr[a, 0:kt, :])
            l = acc[:, free0[a]:free0[a] + 1]
            outs.append(acc * (1.0 / l))
            slot = 1 - slot
        o_ref[q_rows, :] = jnp.where(own[0], outs[0], outs[1]).astype(o_ref.dtype)

    attend(slice(0, N_META), N_META, [(0, LANES, True)], 0)
    for i in range(n_qblk):
        tiles = [(0, LANES, False)]
        tiles += [(LANES + j * tile, tile, j == i) for j in range(i + 1)]
        attend(slice(N_META + i * tile, N_META + (i + 1) * tile), tile, tiles, 0)


def _fox(fox_qkv, qaug, kaug, *, batch, tile):
    n, cols = fox_qkv.shape
    seq_l = n // batch
    seq = seq_l - N_META
    width = cols // 3
    n_pairs = width // LANES
    qkv3 = fox_qkv.reshape(batch, seq_l, cols)
    kern = functools.partial(_fox_kernel, seq=seq, tile=tile)

    def col_block(j0):
        return pl.BlockSpec((None, seq_l, LANES), lambda b, p: (b, 0, j0 + p))

    return pl.pallas_call(
        kern,
        grid=(batch, n_pairs),
        in_specs=[col_block(0), col_block(n_pairs), col_block(2 * n_pairs), col_block(0), col_block(0)],
        out_specs=col_block(0),
        out_shape=jax.ShapeDtypeStruct((batch, seq_l, width), BF16),
        scratch_shapes=[
            pltpu.VMEM((2, LANES + seq, LANES), BF16),
            pltpu.VMEM((2, LANES + seq, LANES), BF16),
            pltpu.VMEM((2, tile, LANES + seq), F32),
            pltpu.VMEM((2, tile, LANES + seq), BF16),
        ],
        compiler_params=pltpu.CompilerParams(
            dimension_semantics=("arbitrary", "arbitrary"), vmem_limit_bytes=VMEM_LIMIT_BYTES),
        name="fox",
    )(qkv3, qkv3, qkv3, qaug, kaug)


def _ret_kernel(q_ref, k_ref, v_ref, g_ref, cs_ref, sn_ref, gn_ref, o_ref, state_scr, dec_scr,
                *, n_chunks):
    C = RET_CHUNK
    ri = lax.broadcasted_iota(jnp.int32, (C, C), 0)
    ci = lax.broadcasted_iota(jnp.int32, (C, C), 1)
    diff = ri - ci
    diff_f = jnp.maximum(diff, 0).astype(F32)
    j_idx = lax.broadcasted_iota(jnp.int32, (C, 1), 0).astype(F32)
    k_scale = RET_HEAD_DIM ** -0.5

    def rot(x, cs, sn):
        return x * cs + pltpu.roll(x, RET_HEAD_DIM // 2, axis=1) * sn

    def head_inputs(rows, hh, pad_rows):
        sl = slice(hh * RET_HEAD_DIM, (hh + 1) * RET_HEAD_DIM)
        q = q_ref[rows, sl].astype(F32)
        k = k_ref[rows, sl].astype(F32)
        v = v_ref[rows, sl]
        cs = cs_ref[rows, :]
        sn = sn_ref[rows, :]
        q = rot(q, cs, sn)
        k = rot(k, cs, sn) * k_scale
        if pad_rows:
            q = jnp.concatenate([q, jnp.zeros((pad_rows, RET_HEAD_DIM), F32)], axis=0)
            k = jnp.concatenate([k, jnp.zeros((pad_rows, RET_HEAD_DIM), F32)], axis=0)
            v = jnp.concatenate([v, jnp.zeros((pad_rows, RET_HEAD_DIM), v.dtype)], axis=0)
        return q, k, v

    def finish(o, rows, hh, n_rows):
        sl = slice(hh * RET_HEAD_DIM, (hh + 1) * RET_HEAD_DIM)
        o = o[0:n_rows, :]
        mu = jnp.mean(o, axis=-1, keepdims=True)
        d = o - mu
        var = jnp.mean(d * d, axis=-1, keepdims=True)
        y = d * lax.rsqrt(var + EPS) * gn_ref[:, sl]
        g = g_ref[rows, sl].astype(F32)
        o_ref[rows, sl] = (g * (1.0 / (1.0 + jnp.exp(-g))) * y).astype(o_ref.dtype)

    row_f = ri.astype(F32)
    for hh in range(RET_HEADS):
        log_g = RET_LOG_GAMMA[hh]
        dec_scr[hh, 0] = jnp.where(diff >= 0, jnp.exp(diff_f * log_g), 0.0)
        dec_scr[hh, 1] = jnp.exp((row_f + 1.0) * log_g)
        dec_scr[hh, 2] = jnp.exp((C - 1.0 - row_f) * log_g)

    def intra(q, k, v, hh):
        s = _dot_nt(q.astype(BF16), k.astype(BF16)) * dec_scr[hh, 0]
        return _dot(s.astype(BF16), v)

    rows_m = slice(0, N_META)
    for hh in range(RET_HEADS):
        q, k, v = head_inputs(rows_m, hh, C - N_META)
        finish(intra(q, k, v, hh), rows_m, hh, N_META)
        k_dec = jnp.exp((N_META - 1.0 - j_idx) * RET_LOG_GAMMA[hh])
        state_scr[hh] = _dot_tn((k * k_dec).astype(BF16), v)

    for c in range(n_chunks):
        rows = slice(N_META + c * C, N_META + (c + 1) * C)
        for hh in range(RET_HEADS):
            q, k, v = head_inputs(rows, hh, 0)
            chunk_dec = jnp.exp(jnp.full((1, 1), C * RET_LOG_GAMMA[hh], F32))
            state = state_scr[hh]
            o = intra(q, k, v, hh) + _dot((q * dec_scr[hh, 1]).astype(BF16), state.astype(BF16))
            state_scr[hh] = chunk_dec * state + _dot_tn((k * dec_scr[hh, 2]).astype(BF16), v)
            finish(o, rows, hh, C)


def _retention(ret_qkvg, cs, sn, gn, *, batch):
    n, cols = ret_qkvg.shape
    seq_l = n // batch
    width = cols // 4
    x3 = ret_qkvg.reshape(batch, seq_l, cols)
    kern = functools.partial(_ret_kernel, n_chunks=(seq_l - N_META) // RET_CHUNK)

    def part(j):
        return pl.BlockSpec((None, seq_l, width), lambda b: (b, 0, j))

    return pl.pallas_call(
        kern,
        grid=(batch,),
        in_specs=[part(0), part(1), part(2), part(3),
                  _const_spec(cs.shape), _const_spec(sn.shape), _const_spec(gn.shape)],
        out_specs=pl.BlockSpec((None, seq_l, width), lambda b: (b, 0, 0)),
        out_shape=jax.ShapeDtypeStruct((batch, seq_l, width), BF16),
        scratch_shapes=[pltpu.VMEM((RET_HEADS, RET_HEAD_DIM, RET_HEAD_DIM), F32),
                        pltpu.VMEM((RET_HEADS, 3, RET_CHUNK, RET_CHUNK), F32)],
        compiler_params=pltpu.CompilerParams(
            dimension_semantics=("arbitrary",), vmem_limit_bytes=VMEM_LIMIT_BYTES),
        name="retention",
    )(x3, x3, x3, x3, cs, sn, gn)


def _outffn_kernel(h_ref, fox_ref, ret_ref, wof_ref, wor_ref, g2_ref, wg_ref, wu_ref, wd_ref,
                   gf_ref, o_ref, hn_scr, h1_scr, act_scr, *, ff_chunk, final):
    h1 = h_ref[...] + _dot(fox_ref[...], wof_ref[...]) + _dot(ret_ref[...], wor_ref[...])
    hn_scr[...] = _rmsnorm(h1, g2_ref[...]).astype(BF16)
    h1_scr[...] = h1
    for c0 in range(0, wg_ref.shape[1], ff_chunk):
        hn = hn_scr[...]
        gate = _dot(hn, wg_ref[:, c0:c0 + ff_chunk])
        up = _dot(hn, wu_ref[:, c0:c0 + ff_chunk])
        act_scr[:, c0:c0 + ff_chunk] = (gate * (1.0 / (1.0 + jnp.exp(-gate))) * up).astype(BF16)
    out = h1_scr[...] + _dot(act_scr[...], wd_ref[...])
    if final:
        out = _rmsnorm(out, gf_ref[...])
    o_ref[...] = out


def _outffn(h, fox_out, ret_out, wof, wor, g2, wg, wu, wd, gf, *, tm, final):
    n, d = h.shape
    d_ff = wg.shape[1]
    kern = functools.partial(_outffn_kernel, ff_chunk=FF_CHUNK, final=final)
    return pl.pallas_call(
        kern,
        grid=(n // tm,),
        in_specs=[
            pl.BlockSpec((tm, d), lambda i: (i, 0)),
            pl.BlockSpec((tm, fox_out.shape[1]), lambda i: (i, 0)),
            pl.BlockSpec((tm, ret_out.shape[1]), lambda i: (i, 0)),
            _const_spec(wof.shape), _const_spec(wor.shape), _const_spec(g2.shape),
            _const_spec(wg.shape), _const_spec(wu.shape), _const_spec(wd.shape),
            _const_spec(gf.shape),
        ],
        out_specs=pl.BlockSpec((tm, d), lambda i: (i, 0)),
        out_shape=jax.ShapeDtypeStruct((n, d), F32),
        scratch_shapes=[pltpu.VMEM((tm, d), BF16),
                        pltpu.VMEM((tm, d), F32),
                        pltpu.VMEM((tm, d_ff), BF16)],
        compiler_params=pltpu.CompilerParams(
            dimension_semantics=("arbitrary",), vmem_limit_bytes=VMEM_LIMIT_BYTES),
        name="outffn",
    )(h, fox_out, ret_out, wof, wor, g2, wg, wu, wd, gf)


FF_CHUNK = 256


def kernel(x, meta_tokens, attn_norm, w_in, b_fgate, ret_gn, w_out, ffn_norm,
           w_gate, w_up, w_down, final_norm):
    batch, seq, d = x.shape
    depth = w_in.shape[0]
    fox_heads = b_fgate.shape[1]
    fox_w = fox_heads * FOX_HEAD_DIM
    ret_w = RET_HEADS * RET_HEAD_DIM
    d_ff = w_gate.shape[2]
    seq_l = seq + N_META
    n = batch * seq_l
    assert w_in.shape[2] == 3 * fox_w + fox_heads + 4 * ret_w
    assert fox_w % LANES == 0 and fox_heads <= LANES and d_ff % FF_CHUNK == 0
    attn_tile = 256 if seq % 256 == 0 else LANES
    assert seq % attn_tile == 0 and seq % RET_CHUNK == 0
    tm = _row_tile(n, 768)

    meta = jnp.broadcast_to(meta_tokens[None].astype(x.dtype), (batch, N_META, d))
    h = jnp.concatenate([meta, x], axis=1).reshape(n, d)

    inv_freq = ROPE_BASE ** (-jnp.arange(0, RET_HEAD_DIM, 2, dtype=F32) / RET_HEAD_DIM)
    ang = jnp.arange(seq_l, dtype=F32)[:, None] * inv_freq[None, :]
    cos, sin = jnp.cos(ang), jnp.sin(ang)
    cs = jnp.concatenate([cos, cos], axis=1)
    sn = jnp.concatenate([-sin, sin], axis=1)

    sel = jnp.asarray(_bias_selector(fox_w // LANES), BF16)
    gf = final_norm.reshape(1, d)
    for i in range(depth):
        w = w_in[i]
        wf = w[:, :3 * fox_w].astype(BF16)
        wl = jnp.pad(w[:, 3 * fox_w:3 * fox_w + fox_heads], ((0, 0), (0, LANES - fox_heads))).astype(BF16)
        wr = w[:, 3 * fox_w + fox_heads:].astype(BF16)
        bl = jnp.pad(b_fgate[i], (0, LANES - fox_heads)).reshape(1, LANES)
        fox_qkv, ret_qkvg, logf = _inproj(h, attn_norm[i].reshape(1, d), wf, wr, wl, bl, tm=tm)

        qaug, kaug = _decay(logf, sel, batch=batch)
        fox_out = _fox(fox_qkv, qaug, kaug, batch=batch, tile=attn_tile)
        ret_out = _retention(ret_qkvg, cs, sn, ret_gn[i].reshape(1, ret_w), batch=batch)

        wo = w_out[i].astype(BF16)
        h = _outffn(h, fox_out.reshape(n, fox_w), ret_out.reshape(n, ret_w),
                    wo[:fox_w], wo[fox_w:], ffn_norm[i].reshape(1, d),
                    w_gate[i].astype(BF16), w_up[i].astype(BF16), w_down[i].astype(BF16), gf,
                    tm=tm, final=(i == depth - 1))
    return h.reshape(batch, seq_l, d)[:, N_META:]
```

```python
import functools
import math

import numpy as np
import jax
import jax.numpy as jnp
from jax import lax
from jax.experimental import pallas as pl
from jax.experimental.pallas import tpu as pltpu

N_META = 16
FOX_HEAD_DIM = 64
RET_HEADS = 4
RET_HEAD_DIM = 128
RET_CHUNK = 128
ROPE_BASE = 10000.0
EPS = 1e-6
RET_LOG_GAMMA = tuple(math.log(1.0 - 2.0 ** (-5 - h)) for h in range(RET_HEADS))
LOG2E = math.log2(math.e)

LANES = 128
BF16_SUBLANES = 16
VMEM_LIMIT_BYTES = 56 * 1024 * 1024

N_SPLIT = 3
MASKED_SCORE = -1e30

F32 = jnp.float32
BF16 = jnp.bfloat16


def _dot(a, b):
    return jnp.dot(a, b, preferred_element_type=F32)


def _dot_nt(a, b):
    return lax.dot_general(a, b, (((1,), (1,)), ((), ())), preferred_element_type=F32)


def _dot_tn(a, b):
    return lax.dot_general(a, b, (((0,), (0,)), ((), ())), preferred_element_type=F32)


def _rmsnorm(x, g):
    return x * lax.rsqrt(jnp.mean(x * x, axis=-1, keepdims=True) + EPS) * g


def _row_tile(n, cap):
    best = None
    for t in range(BF16_SUBLANES, min(n, cap) + 1, BF16_SUBLANES):
        if n % t == 0:
            best = t
    assert best is not None, n
    return best


def _const_spec(shape):
    nd = len(shape)
    return pl.BlockSpec(shape, lambda *_: (0,) * nd, pipeline_mode=pl.Buffered(1))


def _inproj_kernel(h_ref, g_ref, wf_ref, wr_ref, wl_ref, bl_ref,
                   fox_ref, ret_ref, logf_ref, *, fox_w, col_chunk):
    xn = _rmsnorm(h_ref[...], g_ref[...]).astype(BF16)
    q_scale = FOX_HEAD_DIM ** -0.5 * LOG2E
    for c0 in range(0, 3 * fox_w, col_chunk):
        y = _dot(xn, wf_ref[:, c0:c0 + col_chunk])
        if c0 < fox_w:
            y = y * q_scale
        fox_ref[:, c0:c0 + col_chunk] = y.astype(BF16)
    for c0 in range(0, wr_ref.shape[1], col_chunk):
        ret_ref[:, c0:c0 + col_chunk] = _dot(xn, wr_ref[:, c0:c0 + col_chunk]).astype(BF16)
    z = _dot(xn, wl_ref[...]) + bl_ref[...]
    logf_ref[...] = jnp.minimum(z, 0.0) - jnp.log(1.0 + jnp.exp(-jnp.abs(z)))


def _inproj(h, g, wf, wr, wl, bl, *, tm):
    n, d = h.shape
    fox_cols, ret_cols = wf.shape[1], wr.shape[1]
    kern = functools.partial(_inproj_kernel, fox_w=fox_cols // 3, col_chunk=512)
    return pl.pallas_call(
        kern,
        grid=(n // tm,),
        in_specs=[
            pl.BlockSpec((tm, d), lambda i: (i, 0)),
            _const_spec((1, d)),
            _const_spec(wf.shape),
            _const_spec(wr.shape),
            _const_spec(wl.shape),
            _const_spec((1, LANES)),
        ],
        out_specs=[
            pl.BlockSpec((tm, fox_cols), lambda i: (i, 0)),
            pl.BlockSpec((tm, ret_cols), lambda i: (i, 0)),
            pl.BlockSpec((tm, LANES), lambda i: (i, 0)),
        ],
        out_shape=[
            jax.ShapeDtypeStruct((n, fox_cols), BF16),
            jax.ShapeDtypeStruct((n, ret_cols), BF16),
            jax.ShapeDtypeStruct((n, LANES), F32),
        ],
        compiler_params=pltpu.CompilerParams(
            dimension_semantics=("arbitrary",), vmem_limit_bytes=VMEM_LIMIT_BYTES),
        name="inproj",
    )(h, g, wf, wr, wl, bl)


def _split3(x):
    hi = x.astype(BF16)
    r1 = x - hi.astype(F32)
    mid = r1.astype(BF16)
    lo = (r1 - mid.astype(F32)).astype(BF16)
    return hi, mid, lo


def _bias_selector(n_pairs):
    sel = np.zeros((N_SPLIT * LANES, n_pairs * LANES), np.float32)
    for hp in range(n_pairs):
        for a, base in ((1, 0), (0, FOX_HEAD_DIM)):
            head = 2 * hp + a
            for p in range(N_SPLIT):
                sel[p * LANES + head, hp * LANES + base + p] = 1.0
                sel[p * LANES + head, hp * LANES + base + N_SPLIT + p] = -1.0
    return sel


def _decay_kernel(logf_ref, sel_ref, qaug_ref, kaug_ref, c_scr):
    blk = LANES
    ri = lax.broadcasted_iota(jnp.int32, (blk, blk), 0)
    ci = lax.broadcasted_iota(jnp.int32, (blk, blk), 1)
    tri = (ci <= ri).astype(BF16)

    def prefix(x):
        hi, mid, lo = _split3(x)
        return _dot(tri, hi) + _dot(tri, mid) + _dot(tri, lo)

    seq_l = c_scr.shape[0]
    carry = None
    for r0 in range(0, seq_l, blk):
        rows = min(blk, seq_l - r0)
        x = logf_ref[r0:r0 + rows, :]
        if rows < blk:
            x = jnp.concatenate([x, jnp.zeros((blk - rows, LANES), F32)], axis=0)
        cb = prefix(x)
        if carry is not None:
            cb = cb + carry
        c_scr[r0:r0 + rows, :] = cb[0:rows, :]
        carry = cb[blk - 1:blk, :]

    hi, mid, lo = _split3(c_scr[...] * LOG2E)
    routed = _dot(jnp.concatenate([hi, mid, lo], axis=1), sel_ref[...])
    lane = lax.broadcasted_iota(jnp.int32, (1, routed.shape[1]), 1) % FOX_HEAD_DIM
    first = lane < N_SPLIT
    second = (lane >= N_SPLIT) & (lane < 2 * N_SPLIT)
    qaug_ref[...] = jnp.where(first, routed, jnp.where(second, 1.0, 0.0)).astype(BF16)
    kaug_ref[...] = jnp.where(second, routed, jnp.where(first, 1.0, 0.0)).astype(BF16)


def _decay(logf, sel, *, batch):
    n = logf.shape[0]
    seq_l = n // batch
    width = sel.shape[1]
    logf3 = logf.reshape(batch, seq_l, LANES)
    out = pl.BlockSpec((None, seq_l, width), lambda b: (b, 0, 0))
    return pl.pallas_call(
        _decay_kernel,
        grid=(batch,),
        in_specs=[pl.BlockSpec((None, seq_l, LANES), lambda b: (b, 0, 0)), _const_spec(sel.shape)],
        out_specs=[out, out],
        out_shape=[jax.ShapeDtypeStruct((batch, seq_l, width), BF16)] * 2,
        scratch_shapes=[pltpu.VMEM((seq_l, LANES), F32)],
        compiler_params=pltpu.CompilerParams(
            dimension_semantics=("arbitrary",), vmem_limit_bytes=VMEM_LIMIT_BYTES),
        name="decay",
    )(logf3, sel)


def _fox_kernel(q_ref, k_ref, v_ref, qaug_ref, kaug_ref, o_ref,
                k_scr, vt_scr, vm_scr, s_scr, p_scr, *, seq, tile, score_rows):
    seq_l = seq + N_META
    n_qblk = seq // tile
    pad_rows = LANES - N_META
    lane = lax.broadcasted_iota(jnp.int32, (1, LANES), 1)
    own = [lane < FOX_HEAD_DIM, lane >= FOX_HEAD_DIM]
    free0 = [FOX_HEAD_DIM, 0]

    for a in (0, 1):
        ones_col = jnp.where(lane == free0[a], 1.0, 0.0).astype(BF16)
        pad_key = jnp.where(lane == free0[a] + N_SPLIT, MASKED_SCORE, 0.0).astype(BF16)
        k_scr[a, 0:N_META, :] = jnp.where(own[a], k_ref[0:N_META, :], kaug_ref[0:N_META, :])
        k_scr[a, N_META:LANES, :] = jnp.broadcast_to(pad_key, (pad_rows, LANES))
        v_meta = jnp.concatenate([jnp.where(own[a], v_ref[0:N_META, :], ones_col),
                                  jnp.zeros((pad_rows, LANES), BF16)], axis=0)
        vm_scr[a] = v_meta
        vt_scr[a, :, 0:LANES] = v_meta.T
        for r0 in range(0, seq, tile):
            src = slice(N_META + r0, N_META + r0 + tile)
            dst = slice(LANES + r0, LANES + r0 + tile)
            k_scr[a, dst, :] = jnp.where(own[a], k_ref[src, :], kaug_ref[src, :])
            vt_scr[a, :, dst] = jnp.where(own[a], v_ref[src, :], ones_col).T

    qm = q_ref[0:N_META, :]
    qam = qaug_ref[0:N_META, :]
    row_m = lax.broadcasted_iota(jnp.int32, (N_META, LANES), 0)
    col_m = lax.broadcasted_iota(jnp.int32, (N_META, LANES), 1)
    outs = []
    for a in (0, 1):
        s = _dot_nt(jnp.where(own[a], qm, qam), k_scr[a, 0:LANES, :])
        s = jnp.where(col_m <= row_m, s, -jnp.inf)
        p = jnp.exp2(s - jnp.max(s, axis=1, keepdims=True)).astype(BF16)
        acc = _dot(p, vm_scr[a])
        outs.append(acc * (1.0 / acc[:, free0[a]:free0[a] + 1]))
    o_ref[0:N_META, :] = jnp.where(own[0], outs[0], outs[1]).astype(o_ref.dtype)

    n_slots = s_scr.shape[0]

    def geometry(i):
        d0 = LANES + i * tile
        return slice(N_META + i * tile, N_META + (i + 1) * tile), d0, d0 + tile

    def score_stage(i, a):
        q_rows, d0, kt = geometry(i)
        slot = (2 * i + a) % n_slots
        qh = jnp.where(own[a], q_ref[q_rows, :], qaug_ref[q_rows, :])
        mx = None
        for r0 in range(0, kt, score_rows):
            rows = min(score_rows, kt - r0)
            s = _dot_nt(k_scr[a, r0:r0 + rows, :], qh)
            if r0 + rows > d0:
                key_i = lax.broadcasted_iota(jnp.int32, (rows, tile), 0) + (r0 - d0)
                qry_i = lax.broadcasted_iota(jnp.int32, (rows, tile), 1)
                s = jnp.where(key_i <= qry_i, s, -jnp.inf)
            s_scr[slot, r0:r0 + rows, :] = s
            part = jnp.max(s.reshape(rows // 8, 8, tile), axis=0)
            mx = part if mx is None else jnp.maximum(mx, part)
        return jnp.max(mx, axis=0, keepdims=True)

    def value_stage(i, a, m):
        _, _, kt = geometry(i)
        slot = (2 * i + a) % n_slots
        for r0 in range(0, kt, LANES):
            p_scr[slot, r0:r0 + LANES, :] = jnp.exp2(s_scr[slot, r0:r0 + LANES, :] - m).astype(BF16)
        acc_t = _dot(vt_scr[a, :, 0:kt], p_scr[slot, 0:kt, :])
        l = acc_t[free0[a]:free0[a] + 1, :]
        return (acc_t * (1.0 / l)).T

    maxes = [score_stage(0, a) for a in (0, 1)]
    for i in range(n_qblk):
        nxt = [score_stage(i + 1, a) for a in (0, 1)] if i + 1 < n_qblk else None
        outs = [value_stage(i, a, maxes[a]) for a in (0, 1)]
        o_ref[geometry(i)[0], :] = jnp.where(own[0], outs[0], outs[1]).astype(o_ref.dtype)
        maxes = nxt


FOX_SCORE_ROWS = 512
FOX_SLOTS = 4


def _fox(fox_qkv, qaug, kaug, *, batch, tile):
    n, cols = fox_qkv.shape
    seq_l = n // batch
    seq = seq_l - N_META
    width = cols // 3
    n_pairs = width // LANES
    qkv3 = fox_qkv.reshape(batch, seq_l, cols)
    kern = functools.partial(_fox_kernel, seq=seq, tile=tile, score_rows=FOX_SCORE_ROWS)

    def col_block(j0):
        return pl.BlockSpec((None, seq_l, LANES), lambda b, p: (b, 0, j0 + p))

    return pl.pallas_call(
        kern,
        grid=(batch, n_pairs),
        in_specs=[col_block(0), col_block(n_pairs), col_block(2 * n_pairs), col_block(0), col_block(0)],
        out_specs=col_block(0),
        out_shape=jax.ShapeDtypeStruct((batch, seq_l, width), BF16),
        scratch_shapes=[
            pltpu.VMEM((2, LANES + seq, LANES), BF16),
            pltpu.VMEM((2, LANES, LANES + seq), BF16),
            pltpu.VMEM((2, LANES, LANES), BF16),
            pltpu.VMEM((FOX_SLOTS, LANES + seq, tile), F32),
            pltpu.VMEM((FOX_SLOTS, LANES + seq, tile), BF16),
        ],
        compiler_params=pltpu.CompilerParams(
            dimension_semantics=("arbitrary", "arbitrary"), vmem_limit_bytes=VMEM_LIMIT_BYTES),
        name="fox",
    )(qkv3, qkv3, qkv3, qaug, kaug)


def _ret_kernel(q_ref, k_ref, v_ref, g_ref, cs_ref, sn_ref, gn_ref, o_ref, state_scr, dec_scr,
                *, n_chunks):
    C = RET_CHUNK
    ri = lax.broadcasted_iota(jnp.int32, (C, C), 0)
    ci = lax.broadcasted_iota(jnp.int32, (C, C), 1)
    diff = ri - ci
    diff_f = jnp.maximum(diff, 0).astype(F32)
    j_idx = lax.broadcasted_iota(jnp.int32, (C, 1), 0).astype(F32)
    k_scale = RET_HEAD_DIM ** -0.5

    def rot(x, cs, sn):
        return x * cs + pltpu.roll(x, RET_HEAD_DIM // 2, axis=1) * sn

    def head_inputs(rows, hh, pad_rows):
        sl = slice(hh * RET_HEAD_DIM, (hh + 1) * RET_HEAD_DIM)
        q = q_ref[rows, sl].astype(F32)
        k = k_ref[rows, sl].astype(F32)
        v = v_ref[rows, sl]
        cs = cs_ref[rows, :]
        sn = sn_ref[rows, :]
        q = rot(q, cs, sn)
        k = rot(k, cs, sn) * k_scale
        if pad_rows:
            q = jnp.concatenate([q, jnp.zeros((pad_rows, RET_HEAD_DIM), F32)], axis=0)
            k = jnp.concatenate([k, jnp.zeros((pad_rows, RET_HEAD_DIM), F32)], axis=0)
            v = jnp.concatenate([v, jnp.zeros((pad_rows, RET_HEAD_DIM), v.dtype)], axis=0)
        return q, k, v

    def finish(o, rows, hh, n_rows):
        sl = slice(hh * RET_HEAD_DIM, (hh + 1) * RET_HEAD_DIM)
        o = o[0:n_rows, :]
        mu = jnp.mean(o, axis=-1, keepdims=True)
        d = o - mu
        var = jnp.mean(d * d, axis=-1, keepdims=True)
        y = d * lax.rsqrt(var + EPS) * gn_ref[:, sl]
        g = g_ref[rows, sl].astype(F32)
        o_ref[rows, sl] = (g * (1.0 / (1.0 + jnp.exp(-g))) * y).astype(o_ref.dtype)

    row_f = ri.astype(F32)
    for hh in range(RET_HEADS):
        log_g = RET_LOG_GAMMA[hh]
        dec_scr[hh, 0] = jnp.where(diff >= 0, jnp.exp(diff_f * log_g), 0.0)
        dec_scr[hh, 1] = jnp.exp((row_f + 1.0) * log_g)
        dec_scr[hh, 2] = jnp.exp((C - 1.0 - row_f) * log_g)

    def intra(q, k, v, hh):
        s = _dot_nt(q.astype(BF16), k.astype(BF16)) * dec_scr[hh, 0]
        return _dot(s.astype(BF16), v)

    rows_m = slice(0, N_META)
    for hh in range(RET_HEADS):
        q, k, v = head_inputs(rows_m, hh, C - N_META)
        finish(intra(q, k, v, hh), rows_m, hh, N_META)
        k_dec = jnp.exp((N_META - 1.0 - j_idx) * RET_LOG_GAMMA[hh])
        state_scr[hh] = _dot_tn((k * k_dec).astype(BF16), v)

    def chunk_rows(c):
        return slice(N_META + c * C, N_META + (c + 1) * C)

    def matmul_stage(c):
        outs = []
        for hh in range(RET_HEADS):
            q, k, v = head_inputs(chunk_rows(c), hh, 0)
            chunk_dec = jnp.exp(jnp.full((1, 1), C * RET_LOG_GAMMA[hh], F32))
            state = state_scr[hh]
            outs.append(intra(q, k, v, hh) + _dot((q * dec_scr[hh, 1]).astype(BF16), state.astype(BF16)))
            state_scr[hh] = chunk_dec * state + _dot_tn((k * dec_scr[hh, 2]).astype(BF16), v)
        return outs

    outs = matmul_stage(0)
    for c in range(n_chunks):
        nxt = matmul_stage(c + 1) if c + 1 < n_chunks else None
        for hh in range(RET_HEADS):
            finish(outs[hh], chunk_rows(c), hh, C)
        outs = nxt


def _retention(ret_qkvg, cs, sn, gn, *, batch):
    n, cols = ret_qkvg.shape
    seq_l = n // batch
    width = cols // 4
    x3 = ret_qkvg.reshape(batch, seq_l, cols)
    kern = functools.partial(_ret_kernel, n_chunks=(seq_l - N_META) // RET_CHUNK)

    def part(j):
        return pl.BlockSpec((None, seq_l, width), lambda b: (b, 0, j))

    return pl.pallas_call(
        kern,
        grid=(batch,),
        in_specs=[part(0), part(1), part(2), part(3),
                  _const_spec(cs.shape), _const_spec(sn.shape), _const_spec(gn.shape)],
        out_specs=pl.BlockSpec((None, seq_l, width), lambda b: (b, 0, 0)),
        out_shape=jax.ShapeDtypeStruct((batch, seq_l, width), BF16),
        scratch_shapes=[pltpu.VMEM((RET_HEADS, RET_HEAD_DIM, RET_HEAD_DIM), F32),
                        pltpu.VMEM((RET_HEADS, 3, RET_CHUNK, RET_CHUNK), F32)],
        compiler_params=pltpu.CompilerParams(
            dimension_semantics=("arbitrary",), vmem_limit_bytes=VMEM_LIMIT_BYTES),
        name="retention",
    )(x3, x3, x3, x3, cs, sn, gn)


def _outffn_kernel(h_ref, fox_ref, ret_ref, wof_ref, wor_ref, g2_ref, wg_ref, wu_ref, wd_ref,
                   gf_ref, o_ref, hn_scr, h1_scr, act_scr, *, ff_chunk, final):
    h1 = h_ref[...] + _dot(fox_ref[...], wof_ref[...]) + _dot(ret_ref[...], wor_ref[...])
    hn_scr[...] = _rmsnorm(h1, g2_ref[...]).astype(BF16)
    h1_scr[...] = h1
    for c0 in range(0, wg_ref.shape[1], ff_chunk):
        hn = hn_scr[...]
        gate = _dot(hn, wg_ref[:, c0:c0 + ff_chunk])
        up = _dot(hn, wu_ref[:, c0:c0 + ff_chunk])
        act_scr[:, c0:c0 + ff_chunk] = (gate * (1.0 / (1.0 + jnp.exp(-gate))) * up).astype(BF16)
    out = h1_scr[...] + _dot(act_scr[...], wd_ref[...])
    if final:
        out = _rmsnorm(out, gf_ref[...])
    o_ref[...] = out


def _outffn(h, fox_out, ret_out, wof, wor, g2, wg, wu, wd, gf, *, tm, final):
    n, d = h.shape
    d_ff = wg.shape[1]
    kern = functools.partial(_outffn_kernel, ff_chunk=FF_CHUNK, final=final)
    return pl.pallas_call(
        kern,
        grid=(n // tm,),
        in_specs=[
            pl.BlockSpec((tm, d), lambda i: (i, 0)),
            pl.BlockSpec((tm, fox_out.shape[1]), lambda i: (i, 0)),
            pl.BlockSpec((tm, ret_out.shape[1]), lambda i: (i, 0)),
            _const_spec(wof.shape), _const_spec(wor.shape), _const_spec(g2.shape),
            _const_spec(wg.shape), _const_spec(wu.shape), _const_spec(wd.shape),
            _const_spec(gf.shape),
        ],
        out_specs=pl.BlockSpec((tm, d), lambda i: (i, 0)),
        out_shape=jax.ShapeDtypeStruct((n, d), F32),
        scratch_shapes=[pltpu.VMEM((tm, d), BF16),
                        pltpu.VMEM((tm, d), F32),
                        pltpu.VMEM((tm, d_ff), BF16)],
        compiler_params=pltpu.CompilerParams(
            dimension_semantics=("arbitrary",), vmem_limit_bytes=VMEM_LIMIT_BYTES),
        name="outffn",
    )(h, fox_out, ret_out, wof, wor, g2, wg, wu, wd, gf)


FF_CHUNK = 256


def kernel(x, meta_tokens, attn_norm, w_in, b_fgate, ret_gn, w_out, ffn_norm,
           w_gate, w_up, w_down, final_norm):
    batch, seq, d = x.shape
    depth = w_in.shape[0]
    fox_heads = b_fgate.shape[1]
    fox_w = fox_heads * FOX_HEAD_DIM
    ret_w = RET_HEADS * RET_HEAD_DIM
    d_ff = w_gate.shape[2]
    seq_l = seq + N_META
    n = batch * seq_l
    assert w_in.shape[2] == 3 * fox_w + fox_heads + 4 * ret_w
    assert fox_w % LANES == 0 and fox_heads <= LANES and d_ff % FF_CHUNK == 0
    attn_tile = 256 if seq % 256 == 0 else LANES
    assert seq % attn_tile == 0 and seq % RET_CHUNK == 0
    tm = _row_tile(n, 768)

    meta = jnp.broadcast_to(meta_tokens[None].astype(x.dtype), (batch, N_META, d))
    h = jnp.concatenate([meta, x], axis=1).reshape(n, d)

    inv_freq = ROPE_BASE ** (-jnp.arange(0, RET_HEAD_DIM, 2, dtype=F32) / RET_HEAD_DIM)
    ang = jnp.arange(seq_l, dtype=F32)[:, None] * inv_freq[None, :]
    cos, sin = jnp.cos(ang), jnp.sin(ang)
    cs = jnp.concatenate([cos, cos], axis=1)
    sn = jnp.concatenate([-sin, sin], axis=1)

    sel = jnp.asarray(_bias_selector(fox_w // LANES), BF16)
    gf = final_norm.reshape(1, d)
    for i in range(depth):
        w = w_in[i]
        wf = w[:, :3 * fox_w].astype(BF16)
        wl = jnp.pad(w[:, 3 * fox_w:3 * fox_w + fox_heads], ((0, 0), (0, LANES - fox_heads))).astype(BF16)
        wr = w[:, 3 * fox_w + fox_heads:].astype(BF16)
        bl = jnp.pad(b_fgate[i], (0, LANES - fox_heads)).reshape(1, LANES)
        fox_qkv, ret_qkvg, logf = _inproj(h, attn_norm[i].reshape(1, d), wf, wr, wl, bl, tm=tm)

        qaug, kaug = _decay(logf, sel, batch=batch)
        fox_out = _fox(fox_qkv, qaug, kaug, batch=batch, tile=attn_tile)
        ret_out = _retention(ret_qkvg, cs, sn, ret_gn[i].reshape(1, ret_w), batch=batch)

        wo = w_out[i].astype(BF16)
        h = _outffn(h, fox_out.reshape(n, fox_w), ret_out.reshape(n, ret_w),
                    wo[:fox_w], wo[fox_w:], ffn_norm[i].reshape(1, d),
                    w_gate[i].astype(BF16), w_up[i].astype(BF16), w_down[i].astype(BF16), gf,
                    tm=tm, final=(i == depth - 1))
    return h.reshape(batch, seq_l, d)[:, N_META:]
```

```python
import functools
import math

import numpy as np
import jax
import jax.numpy as jnp
from jax import lax
from jax.experimental import pallas as pl
from jax.experimental.pallas import tpu as pltpu

N_META = 16
FOX_HEAD_DIM = 64
RET_HEADS = 4
RET_HEAD_DIM = 128
RET_CHUNK = 128
ROPE_BASE = 10000.0
EPS = 1e-6
RET_LOG_GAMMA = tuple(math.log(1.0 - 2.0 ** (-5 - h)) for h in range(RET_HEADS))
LOG2E = math.log2(math.e)

LANES = 128
BF16_SUBLANES = 16
VMEM_LIMIT_BYTES = 56 * 1024 * 1024

N_SPLIT = 3
MASKED_SCORE = -1e30

INPROJ_ROWS = 1024
OUTFFN_ROWS = 512
FF_CHUNK = 256
FOX_TILE = 256
FOX_SCORE_ROWS = 512
FOX_SLOTS = 4

F32 = jnp.float32
BF16 = jnp.bfloat16

assert RET_CHUNK == RET_HEAD_DIM == LANES


def _dot(a, b):
    return jnp.dot(a, b, preferred_element_type=F32)


def _dot_nt(a, b):
    return lax.dot_general(a, b, (((1,), (1,)), ((), ())), preferred_element_type=F32)


def _dot_tn(a, b):
    return lax.dot_general(a, b, (((0,), (0,)), ((), ())), preferred_element_type=F32)


def _rmsnorm(x, g):
    return x * lax.rsqrt(jnp.mean(x * x, axis=-1, keepdims=True) + EPS) * g


def _row_tile(n, cap):
    best = None
    for t in range(BF16_SUBLANES, min(n, cap) + 1, BF16_SUBLANES):
        if n % t == 0:
            best = t
    assert best is not None, n
    return best


def _const_spec(shape):
    nd = len(shape)
    return pl.BlockSpec(shape, lambda *_: (0,) * nd, pipeline_mode=pl.Buffered(1))


def _layer_spec(shape, layer):
    nd = len(shape) - 1
    return pl.BlockSpec((None,) + tuple(shape[1:]), lambda *_: (layer,) + (0,) * nd,
                        pipeline_mode=pl.Buffered(1))


def _params(n_axes):
    return pltpu.CompilerParams(dimension_semantics=("arbitrary",) * n_axes,
                                vmem_limit_bytes=VMEM_LIMIT_BYTES)


def _inproj_body(h_ref, cs_ref, sn_ref, g_ref, w_ref, bl_ref, fox_ref, ret_ref, logf_ref,
                 *, fox_w, ret_w):
    xn = _rmsnorm(h_ref[...], g_ref[...]).astype(BF16)
    q_scale = FOX_HEAD_DIM ** -0.5 * LOG2E
    k_scale = RET_HEAD_DIM ** -0.5
    for c0 in range(0, 3 * fox_w, fox_w):
        y = _dot(xn, w_ref[:, c0:c0 + fox_w])
        if c0 == 0:
            y = y * q_scale
        fox_ref[:, c0:c0 + fox_w] = y.astype(BF16)
    base = 3 * fox_w
    cs = cs_ref[...]
    sn = sn_ref[...]
    for part in range(4):
        y = _dot(xn, w_ref[:, base + part * ret_w:base + (part + 1) * ret_w])
        if part < 2:
            for hh in range(RET_HEADS):
                sl = slice(hh * RET_HEAD_DIM, (hh + 1) * RET_HEAD_DIM)
                yh = y[:, sl]
                yh = yh * cs + pltpu.roll(yh, RET_HEAD_DIM // 2, axis=1) * sn
                if part == 1:
                    yh = yh * k_scale
                ret_ref[:, part * ret_w + sl.start:part * ret_w + sl.stop] = yh.astype(BF16)
        else:
            ret_ref[:, part * ret_w:(part + 1) * ret_w] = y.astype(BF16)
    z = _dot(xn, w_ref[:, base + 4 * ret_w:base + 4 * ret_w + LANES]) + bl_ref[...]
    logf_ref[...] = jnp.minimum(z, 0.0) - jnp.log(1.0 + jnp.exp(-jnp.abs(z)))


def _inproj_kernel(hr_ref, hm_ref, csr_ref, snr_ref, csm_ref, snm_ref, g_ref, w_ref, bl_ref,
                   foxr_ref, retr_ref, logfr_ref, foxm_ref, retm_ref, logfm_ref,
                   *, n_tiles, fox_w, ret_w):
    body = functools.partial(_inproj_body, fox_w=fox_w, ret_w=ret_w)
    t = pl.program_id(0)

    @pl.when(t < n_tiles)
    def _():
        body(hr_ref, csr_ref, snr_ref, g_ref, w_ref, bl_ref, foxr_ref, retr_ref, logfr_ref)

    @pl.when(t == n_tiles)
    def _():
        body(hm_ref, csm_ref, snm_ref, g_ref, w_ref, bl_ref, foxm_ref, retm_ref, logfm_ref)


def _inproj(h_r, h_m, cs_r, sn_r, cs_m, sn_m, g, w_all, bl, *, layer, seq, fox_w, ret_w):
    n_r, d = h_r.shape
    n_m = h_m.shape[0]
    tm = _row_tile(seq, INPROJ_ROWS)
    n_tiles = n_r // tm
    per_seq = seq // tm
    fox_cols, ret_cols = 3 * fox_w, 4 * ret_w

    def real(cols):
        return pl.BlockSpec((tm, cols), lambda t: (jnp.minimum(t, n_tiles - 1), 0))

    def table():
        return pl.BlockSpec((tm, LANES), lambda t: (jnp.minimum(t, n_tiles - 1) % per_seq, 0))

    def meta(cols):
        return pl.BlockSpec((n_m, cols), lambda t: (0, 0))

    kern = functools.partial(_inproj_kernel, n_tiles=n_tiles, fox_w=fox_w, ret_w=ret_w)
    return pl.pallas_call(
        kern,
        grid=(n_tiles + 1,),
        in_specs=[real(d), meta(d), table(), table(), meta(LANES), meta(LANES),
                  _const_spec((1, d)), _layer_spec(w_all.shape, layer), _const_spec((1, LANES))],
        out_specs=[real(fox_cols), real(ret_cols), real(LANES),
                   meta(fox_cols), meta(ret_cols), meta(LANES)],
        out_shape=[
            jax.ShapeDtypeStruct((n_r, fox_cols), BF16),
            jax.ShapeDtypeStruct((n_r, ret_cols), BF16),
            jax.ShapeDtypeStruct((n_r, LANES), F32),
            jax.ShapeDtypeStruct((n_m, fox_cols), BF16),
            jax.ShapeDtypeStruct((n_m, ret_cols), BF16),
            jax.ShapeDtypeStruct((n_m, LANES), F32),
        ],
        compiler_params=_params(1),
        name="inproj",
    )(h_r, h_m, cs_r, sn_r, cs_m, sn_m, g, w_all, bl)


def _split3(x):
    hi = x.astype(BF16)
    r1 = x - hi.astype(F32)
    mid = r1.astype(BF16)
    lo = (r1 - mid.astype(F32)).astype(BF16)
    return hi, mid, lo


def _bias_selector(n_pairs):
    sel = np.zeros((N_SPLIT * LANES, n_pairs * LANES), np.float32)
    for hp in range(n_pairs):
        for a, base in ((1, 0), (0, FOX_HEAD_DIM)):
            head = 2 * hp + a
            for p in range(N_SPLIT):
                sel[p * LANES + head, hp * LANES + base + p] = 1.0
                sel[p * LANES + head, hp * LANES + base + N_SPLIT + p] = -1.0
    return sel


def _decay_kernel(logf_ref, logfm_ref, sel_ref, qaug_ref, kaug_ref, qaugm_ref, kaugm_ref, c_scr):
    blk = LANES
    seq = logf_ref.shape[0]
    ri = lax.broadcasted_iota(jnp.int32, (blk, blk), 0)
    ci = lax.broadcasted_iota(jnp.int32, (blk, blk), 1)
    tri = (ci <= ri).astype(BF16)

    def prefix(x):
        hi, mid, lo = _split3(x)
        return _dot(tri, hi) + _dot(tri, mid) + _dot(tri, lo)

    cb = prefix(jnp.concatenate([logfm_ref[...], jnp.zeros((blk - N_META, LANES), F32)], axis=0))
    c_scr[0:blk, :] = cb
    carry = cb[N_META - 1:N_META, :]
    for r0 in range(0, seq, blk):
        cb = prefix(logf_ref[r0:r0 + blk, :]) + carry
        c_scr[blk + r0:2 * blk + r0, :] = cb
        carry = cb[blk - 1:blk, :]

    lane = lax.broadcasted_iota(jnp.int32, (1, sel_ref.shape[1]), 1) % FOX_HEAD_DIM
    first = lane < N_SPLIT
    second = (lane >= N_SPLIT) & (lane < 2 * N_SPLIT)

    def route(c, q_out, k_out):
        hi, mid, lo = _split3(c * LOG2E)
        routed = _dot(jnp.concatenate([hi, mid, lo], axis=1), sel_ref[...])
        q_out[...] = jnp.where(first, routed, jnp.where(second, 1.0, 0.0)).astype(BF16)
        k_out[...] = jnp.where(second, routed, jnp.where(first, 1.0, 0.0)).astype(BF16)

    route(c_scr[0:N_META, :], qaugm_ref, kaugm_ref)
    route(c_scr[blk:, :], qaug_ref, kaug_ref)


def _decay(logf_r, logf_m, sel, *, batch):
    seq = logf_r.shape[0] // batch
    width = sel.shape[1]

    def rows(n, cols):
        return pl.BlockSpec((None, n, cols), lambda b: (b, 0, 0))

    return pl.pallas_call(
        _decay_kernel,
        grid=(batch,),
        in_specs=[rows(seq, LANES), rows(N_META, LANES), _const_spec(sel.shape)],
        out_specs=[rows(seq, width), rows(seq, width), rows(N_META, width), rows(N_META, width)],
        out_shape=[jax.ShapeDtypeStruct((batch, seq, width), BF16)] * 2
        + [jax.ShapeDtypeStruct((batch, N_META, width), BF16)] * 2,
        scratch_shapes=[pltpu.VMEM((LANES + seq, LANES), F32)],
        compiler_params=_params(1),
        name="decay",
    )(logf_r.reshape(batch, seq, LANES), logf_m.reshape(batch, N_META, LANES), sel)


def _fox_kernel(q_ref, k_ref, v_ref, qm_ref, km_ref, vm_ref, qaug_ref, kaug_ref, qaugm_ref, kaugm_ref,
                o_ref, om_ref, k_scr, vt_scr, vm_scr, s_scr, p_scr, *, tile, score_rows):
    seq = q_ref.shape[0]
    n_qblk = seq // tile
    pad_rows = LANES - N_META
    lane = lax.broadcasted_iota(jnp.int32, (1, LANES), 1)
    own = [lane < FOX_HEAD_DIM, lane >= FOX_HEAD_DIM]
    free0 = [FOX_HEAD_DIM, 0]

    for a in (0, 1):
        ones_col = jnp.where(lane == free0[a], 1.0, 0.0).astype(BF16)
        pad_key = jnp.where(lane == free0[a] + N_SPLIT, MASKED_SCORE, 0.0).astype(BF16)
        k_scr[a, 0:N_META, :] = jnp.where(own[a], km_ref[...], kaugm_ref[...])
        k_scr[a, N_META:LANES, :] = jnp.broadcast_to(pad_key, (pad_rows, LANES))
        v_meta = jnp.concatenate([jnp.where(own[a], vm_ref[...], ones_col),
                                  jnp.zeros((pad_rows, LANES), BF16)], axis=0)
        vm_scr[a] = v_meta
        vt_scr[a, :, 0:LANES] = v_meta.T
        for r0 in range(0, seq, tile):
            src = slice(r0, r0 + tile)
            dst = slice(LANES + r0, LANES + r0 + tile)
            k_scr[a, dst, :] = jnp.where(own[a], k_ref[src, :], kaug_ref[src, :])
            vt_scr[a, :, dst] = jnp.where(own[a], v_ref[src, :], ones_col).T

    row_m = lax.broadcasted_iota(jnp.int32, (N_META, LANES), 0)
    col_m = lax.broadcasted_iota(jnp.int32, (N_META, LANES), 1)
    outs = []
    for a in (0, 1):
        s = _dot_nt(jnp.where(own[a], qm_ref[...], qaugm_ref[...]), k_scr[a, 0:LANES, :])
        s = jnp.where(col_m <= row_m, s, -jnp.inf)
        p = jnp.exp2(s - jnp.max(s, axis=1, keepdims=True)).astype(BF16)
        acc = _dot(p, vm_scr[a])
        outs.append(acc * (1.0 / acc[:, free0[a]:free0[a] + 1]))
    om_ref[...] = jnp.where(own[0], outs[0], outs[1]).astype(om_ref.dtype)

    n_slots = s_scr.shape[0]

    def geometry(i):
        d0 = LANES + i * tile
        return slice(i * tile, (i + 1) * tile), d0, d0 + tile

    def score_stage(i, a):
        q_rows, d0, kt = geometry(i)
        slot = (2 * i + a) % n_slots
        qh = jnp.where(own[a], q_ref[q_rows, :], qaug_ref[q_rows, :])
        mx = None
        for r0 in range(0, kt, score_rows):
            rows = min(score_rows, kt - r0)
            s = _dot_nt(k_scr[a, r0:r0 + rows, :], qh)
            if r0 + rows > d0:
                key_i = lax.broadcasted_iota(jnp.int32, (rows, tile), 0) + (r0 - d0)
                qry_i = lax.broadcasted_iota(jnp.int32, (rows, tile), 1)
                s = jnp.where(key_i <= qry_i, s, -jnp.inf)
            s_scr[slot, r0:r0 + rows, :] = s
            part = jnp.max(s.reshape(rows // 8, 8, tile), axis=0)
            mx = part if mx is None else jnp.maximum(mx, part)
        return jnp.max(mx, axis=0, keepdims=True)

    def value_stage(i, a, m):
        _, _, kt = geometry(i)
        slot = (2 * i + a) % n_slots
        for r0 in range(0, kt, LANES):
            p_scr[slot, r0:r0 + LANES, :] = jnp.exp2(s_scr[slot, r0:r0 + LANES, :] - m).astype(BF16)
        acc_t = _dot(vt_scr[a, :, 0:kt], p_scr[slot, 0:kt, :])
        l = acc_t[free0[a]:free0[a] + 1, :]
        return (acc_t * (1.0 / l)).T

    maxes = [score_stage(0, a) for a in (0, 1)]
    for i in range(n_qblk):
        nxt = [score_stage(i + 1, a) for a in (0, 1)] if i + 1 < n_qblk else None
        outs = [value_stage(i, a, maxes[a]) for a in (0, 1)]
        o_ref[geometry(i)[0], :] = jnp.where(own[0], outs[0], outs[1]).astype(o_ref.dtype)
        maxes = nxt


def _fox(fox_r, fox_m, qaug, kaug, qaug_m, kaug_m, *, batch):
    n_r, cols = fox_r.shape
    seq = n_r // batch
    width = cols // 3
    n_pairs = width // LANES
    tile = FOX_TILE
    assert seq % tile == 0
    qkv_r = fox_r.reshape(batch, seq, cols)
    qkv_m = fox_m.reshape(batch, N_META, cols)
    kern = functools.partial(_fox_kernel, tile=tile, score_rows=FOX_SCORE_ROWS)

    def cols_of(rows, j0):
        return pl.BlockSpec((None, rows, LANES), lambda b, p: (b, 0, j0 + p))

    return pl.pallas_call(
        kern,
        grid=(batch, n_pairs),
        in_specs=[cols_of(seq, 0), cols_of(seq, n_pairs), cols_of(seq, 2 * n_pairs),
                  cols_of(N_META, 0), cols_of(N_META, n_pairs), cols_of(N_META, 2 * n_pairs),
                  cols_of(seq, 0), cols_of(seq, 0), cols_of(N_META, 0), cols_of(N_META, 0)],
        out_specs=[cols_of(seq, 0), cols_of(N_META, 0)],
        out_shape=[jax.ShapeDtypeStruct((batch, seq, width), BF16),
                   jax.ShapeDtypeStruct((batch, N_META, width), BF16)],
        scratch_shapes=[
            pltpu.VMEM((2, LANES + seq, LANES), BF16),
            pltpu.VMEM((2, LANES, LANES + seq), BF16),
            pltpu.VMEM((2, LANES, LANES), BF16),
            pltpu.VMEM((FOX_SLOTS, LANES + seq, tile), F32),
            pltpu.VMEM((FOX_SLOTS, LANES + seq, tile), BF16),
        ],
        compiler_params=_params(2),
        name="fox",
    )(qkv_r, qkv_r, qkv_r, qkv_m, qkv_m, qkv_m, qaug, kaug, qaug_m, kaug_m)


def _ret_kernel(q_ref, k_ref, v_ref, g_ref, qm_ref, km_ref, vm_ref, gm_ref, gn_ref,
                o_ref, om_ref, state_scr, dec_scr):
    C = RET_CHUNK
    n_chunks = q_ref.shape[0] // C
    ri = lax.broadcasted_iota(jnp.int32, (C, C), 0)
    ci = lax.broadcasted_iota(jnp.int32, (C, C), 1)
    diff = ri - ci
    row_f = ri.astype(F32)

    for hh in range(RET_HEADS):
        log_g = RET_LOG_GAMMA[hh]
        dec_scr[hh, 0] = jnp.where(diff >= 0, jnp.exp(jnp.maximum(diff, 0).astype(F32) * log_g), 0.0)
        dec_scr[hh, 1] = jnp.exp((row_f + 1.0) * log_g)
        dec_scr[hh, 2] = jnp.exp((C - 1.0 - row_f) * log_g)
        dec_scr[hh, 3] = jnp.exp((N_META - 1.0 - row_f) * log_g)

    def head(hh):
        return slice(hh * RET_HEAD_DIM, (hh + 1) * RET_HEAD_DIM)

    def finish(o, g, out_ref, rows, hh):
        mu = jnp.mean(o, axis=-1, keepdims=True)
        d = o - mu
        var = jnp.mean(d * d, axis=-1, keepdims=True)
        y = d * lax.rsqrt(var + EPS) * gn_ref[:, head(hh)]
        g = g.astype(F32)
        out_ref[rows, head(hh)] = (g * (1.0 / (1.0 + jnp.exp(-g))) * y).astype(out_ref.dtype)

    def intra(q, k, v, hh):
        s = _dot_nt(q, k) * dec_scr[hh, 0]
        return _dot(s.astype(BF16), v)

    def decayed_keys(k, hh, table):
        return (k.astype(F32) * dec_scr[hh, table]).astype(BF16)

    pad = jnp.zeros((C - N_META, RET_HEAD_DIM), BF16)
    for hh in range(RET_HEADS):
        q, k, v = (jnp.concatenate([r[:, head(hh)], pad], axis=0) for r in (qm_ref, km_ref, vm_ref))
        finish(intra(q, k, v, hh)[0:N_META, :], gm_ref[:, head(hh)], om_ref, slice(0, N_META), hh)
        state_scr[hh] = _dot_tn(decayed_keys(k, hh, 3), v)

    def rows_of(c):
        return slice(c * C, (c + 1) * C)

    def scores(c):
        out = []
        for hh in range(RET_HEADS):
            q, k, v = (r[rows_of(c), head(hh)] for r in (q_ref, k_ref, v_ref))
            out.append((_dot_nt(q, k), _dot_tn(decayed_keys(k, hh, 2), v)))
        return out

    def carry(c, kv):
        out = []
        for hh in range(RET_HEADS):
            state = state_scr[hh]
            out.append(_dot(q_ref[rows_of(c), head(hh)], state.astype(BF16)) * dec_scr[hh, 1])
            chunk_dec = jnp.exp(jnp.full((1, 1), C * RET_LOG_GAMMA[hh], F32))
            state_scr[hh] = chunk_dec * state + kv[hh]
        return out

    def outputs(c, s, inter):
        for hh in range(RET_HEADS):
            o = _dot((s[hh] * dec_scr[hh, 0]).astype(BF16), v_ref[rows_of(c), head(hh)]) + inter[hh]
            finish(o, g_ref[rows_of(c), head(hh)], o_ref, rows_of(c), hh)

    cur = scores(0)
    inter = carry(0, [kv for _, kv in cur])
    for c in range(n_chunks):
        nxt = scores(c + 1) if c + 1 < n_chunks else None
        outputs(c, [s for s, _ in cur], inter)
        if nxt is not None:
            inter = carry(c + 1, [kv for _, kv in nxt])
        cur = nxt


def _retention(ret_r, ret_m, gn, *, batch):
    n_r, cols = ret_r.shape
    seq = n_r // batch
    width = cols // 4
    x_r = ret_r.reshape(batch, seq, cols)
    x_m = ret_m.reshape(batch, N_META, cols)

    def part(rows, j):
        return pl.BlockSpec((None, rows, width), lambda b: (b, 0, j))

    return pl.pallas_call(
        _ret_kernel,
        grid=(batch,),
        in_specs=[part(seq, j) for j in range(4)] + [part(N_META, j) for j in range(4)]
        + [_const_spec(gn.shape)],
        out_specs=[part(seq, 0), part(N_META, 0)],
        out_shape=[jax.ShapeDtypeStruct((batch, seq, width), BF16),
                   jax.ShapeDtypeStruct((batch, N_META, width), BF16)],
        scratch_shapes=[pltpu.VMEM((RET_HEADS, RET_HEAD_DIM, RET_HEAD_DIM), F32),
                        pltpu.VMEM((RET_HEADS, 4, RET_CHUNK, RET_HEAD_DIM), F32)],
        compiler_params=_params(1),
        name="retention",
    )(x_r, x_r, x_r, x_r, x_m, x_m, x_m, x_m, gn)


def _outffn_body(h_ref, fox_ref, ret_ref, wo_ref, g2_ref, wg_ref, wu_ref, wd_ref, gf_ref, o_ref,
                 hn_scr, act_scr, *, final):
    rows = h_ref.shape[0]
    fox_w = fox_ref.shape[1]
    h1 = h_ref[...] + _dot(fox_ref[...], wo_ref[0:fox_w, :]) + _dot(ret_ref[...], wo_ref[fox_w:, :])
    hn_scr[0:rows, :] = _rmsnorm(h1, g2_ref[...]).astype(BF16)
    o_ref[...] = h1
    for c0 in range(0, wg_ref.shape[1], FF_CHUNK):
        hn = hn_scr[0:rows, :]
        gate = _dot(hn, wg_ref[:, c0:c0 + FF_CHUNK])
        up = _dot(hn, wu_ref[:, c0:c0 + FF_CHUNK])
        act_scr[0:rows, c0:c0 + FF_CHUNK] = (gate * (1.0 / (1.0 + jnp.exp(-gate))) * up).astype(BF16)
    out = o_ref[...] + _dot(act_scr[0:rows, :], wd_ref[...])
    if final:
        out = _rmsnorm(out, gf_ref[...])
    o_ref[...] = out


def _outffn_kernel(*refs, n_tiles, final):
    if final:
        (hr, foxr, retr, wo, g2, wg, wu, wd, gf, outr, hn_scr, act_scr) = refs
    else:
        (hr, foxr, retr, hm, foxm, retm, wo, g2, wg, wu, wd, gf, outr, outm, hn_scr, act_scr) = refs
    body = functools.partial(_outffn_body, final=final)
    t = pl.program_id(0)

    @pl.when(t < n_tiles)
    def _():
        body(hr, foxr, retr, wo, g2, wg, wu, wd, gf, outr, hn_scr, act_scr)

    if not final:
        @pl.when(t == n_tiles)
        def _():
            body(hm, foxm, retm, wo, g2, wg, wu, wd, gf, outm, hn_scr, act_scr)


def _outffn(h_r, fox_r, ret_r, h_m, fox_m, ret_m, wo, g2, wg, wu, wd, gf, *, layer, final):
    n_r, d = h_r.shape
    n_m = h_m.shape[0]
    d_ff = wg.shape[2]
    tm = _row_tile(n_r, OUTFFN_ROWS)
    n_tiles = n_r // tm
    assert n_m <= tm

    def real(cols):
        return pl.BlockSpec((tm, cols), lambda t: (jnp.minimum(t, n_tiles - 1), 0))

    def meta(cols):
        return pl.BlockSpec((n_m, cols), lambda t: (0, 0))

    weights = [_layer_spec(wo.shape, layer), _const_spec(g2.shape), _layer_spec(wg.shape, layer),
               _layer_spec(wu.shape, layer), _layer_spec(wd.shape, layer), _const_spec(gf.shape)]
    real_in = [real(d), real(fox_r.shape[1]), real(ret_r.shape[1])]
    kern = functools.partial(_outffn_kernel, n_tiles=n_tiles, final=final)
    scratch = [pltpu.VMEM((tm, d), BF16),
               pltpu.VMEM((tm, d_ff), BF16)]
    if final:
        return pl.pallas_call(
            kern, grid=(n_tiles,), in_specs=real_in + weights, out_specs=real(d),
            out_shape=jax.ShapeDtypeStruct((n_r, d), F32), scratch_shapes=scratch,
            compiler_params=_params(1), name="outffn_last",
        )(h_r, fox_r, ret_r, wo, g2, wg, wu, wd, gf), None
    meta_in = [meta(d), meta(fox_m.shape[1]), meta(ret_m.shape[1])]
    return pl.pallas_call(
        kern, grid=(n_tiles + 1,), in_specs=real_in + meta_in + weights,
        out_specs=[real(d), meta(d)],
        out_shape=[jax.ShapeDtypeStruct((n_r, d), F32), jax.ShapeDtypeStruct((n_m, d), F32)],
        scratch_shapes=scratch, compiler_params=_params(1), name="outffn",
    )(h_r, fox_r, ret_r, h_m, fox_m, ret_m, wo, g2, wg, wu, wd, gf)


def kernel(x, meta_tokens, attn_norm, w_in, b_fgate, ret_gn, w_out, ffn_norm,
           w_gate, w_up, w_down, final_norm):
    batch, seq, d = x.shape
    depth = w_in.shape[0]
    fox_heads = b_fgate.shape[1]
    fox_w = fox_heads * FOX_HEAD_DIM
    ret_w = RET_HEADS * RET_HEAD_DIM
    d_ff = w_gate.shape[2]
    assert w_in.shape[2] == 3 * fox_w + fox_heads + 4 * ret_w
    assert fox_w % LANES == 0 and fox_heads <= LANES and d_ff % FF_CHUNK == 0
    assert seq % FOX_TILE == 0 and seq % RET_CHUNK == 0

    h_r = x.reshape(batch * seq, d)
    h_m = jnp.broadcast_to(meta_tokens[None].astype(x.dtype), (batch, N_META, d)).reshape(batch * N_META, d)

    inv_freq = ROPE_BASE ** (-jnp.arange(0, RET_HEAD_DIM, 2, dtype=F32) / RET_HEAD_DIM)
    ang = jnp.arange(seq + N_META, dtype=F32)[:, None] * inv_freq[None, :]
    cos, sin = jnp.cos(ang), jnp.sin(ang)
    cs = jnp.concatenate([cos, cos], axis=1)
    sn = jnp.concatenate([-sin, sin], axis=1)
    cs_r, sn_r = cs[N_META:], sn[N_META:]
    cs_m, sn_m = jnp.tile(cs[:N_META], (batch, 1)), jnp.tile(sn[:N_META], (batch, 1))

    split = 3 * fox_w
    w_all = jnp.concatenate(
        [w_in[..., :split], w_in[..., split + fox_heads:], w_in[..., split:split + fox_heads],
         jnp.zeros((depth, d, LANES - fox_heads), w_in.dtype)], axis=-1).astype(BF16)
    wo, wg, wu, wd = (w.astype(BF16) for w in (w_out, w_gate, w_up, w_down))
    sel = jnp.asarray(_bias_selector(fox_w // LANES), BF16)
    gf = final_norm.reshape(1, d)

    for i in range(depth):
        bl = jnp.pad(b_fgate[i], (0, LANES - fox_heads)).reshape(1, LANES)
        fox_r, ret_r, logf_r, fox_m, ret_m, logf_m = _inproj(
            h_r, h_m, cs_r, sn_r, cs_m, sn_m, attn_norm[i].reshape(1, d), w_all, bl,
            layer=i, seq=seq, fox_w=fox_w, ret_w=ret_w)
        qaug, kaug, qaug_m, kaug_m = _decay(logf_r, logf_m, sel, batch=batch)
        fo_r, fo_m = _fox(fox_r, fox_m, qaug, kaug, qaug_m, kaug_m, batch=batch)
        ro_r, ro_m = _retention(ret_r, ret_m, ret_gn[i].reshape(1, ret_w), batch=batch)
        h_r, h_m = _outffn(h_r, fo_r.reshape(batch * seq, fox_w), ro_r.reshape(batch * seq, ret_w),
                           h_m, fo_m.reshape(batch * N_META, fox_w), ro_m.reshape(batch * N_META, ret_w),
                           wo, ffn_norm[i].reshape(1, d), wg, wu, wd, gf,
                           layer=i, final=(i == depth - 1))
    return h_r.reshape(batch, seq, d)
```

```python
import functools
import math

import numpy as np
import jax
import jax.numpy as jnp
from jax import lax
from jax.experimental import pallas as pl
from jax.experimental.pallas import tpu as pltpu

N_META = 16
FOX_HEAD_DIM = 64
RET_HEADS = 4
RET_HEAD_DIM = 128
RET_CHUNK = 128
ROPE_BASE = 10000.0
EPS = 1e-6
RET_LOG_GAMMA = tuple(math.log(1.0 - 2.0 ** (-5 - h)) for h in range(RET_HEADS))
LOG2E = math.log2(math.e)

LANES = 128
BF16_SUBLANES = 16
VMEM_LIMIT_BYTES = 56 * 1024 * 1024

N_SPLIT = 3
BIAS_LANES = 2 * N_SPLIT

INPROJ_ROWS = 1024
OUTFFN_ROWS = 512
FF_CHUNK = 256
FOX_TILE = 256
FOX_SCORE_ROWS = 512
FOX_SLOTS = 4

F32 = jnp.float32
BF16 = jnp.bfloat16

assert RET_CHUNK == RET_HEAD_DIM == LANES


def _dot(a, b):
    return jnp.dot(a, b, preferred_element_type=F32)


def _dot_nt(a, b):
    return lax.dot_general(a, b, (((1,), (1,)), ((), ())), preferred_element_type=F32)


def _dot_tn(a, b):
    return lax.dot_general(a, b, (((0,), (0,)), ((), ())), preferred_element_type=F32)


def _rmsnorm(x, g):
    return x * lax.rsqrt(jnp.mean(x * x, axis=-1, keepdims=True) + EPS) * g


def _row_tile(n, cap):
    best = None
    for t in range(BF16_SUBLANES, min(n, cap) + 1, BF16_SUBLANES):
        if n % t == 0:
            best = t
    assert best is not None, n
    return best


def _const_spec(shape):
    nd = len(shape)
    return pl.BlockSpec(shape, lambda *_: (0,) * nd, pipeline_mode=pl.Buffered(1))


def _layer_spec(shape, layer):
    nd = len(shape) - 1
    return pl.BlockSpec((None,) + tuple(shape[1:]), lambda *_: (layer,) + (0,) * nd,
                        pipeline_mode=pl.Buffered(1))


def _params(n_axes):
    return pltpu.CompilerParams(dimension_semantics=("arbitrary",) * n_axes,
                                vmem_limit_bytes=VMEM_LIMIT_BYTES)


def _inproj_body(h_ref, cs_ref, sn_ref, g_ref, w_ref, bl_ref, fox_ref, ret_ref, logf_ref,
                 *, fox_w, ret_w):
    xn = _rmsnorm(h_ref[...], g_ref[...]).astype(BF16)
    q_scale = FOX_HEAD_DIM ** -0.5 * LOG2E
    k_scale = RET_HEAD_DIM ** -0.5
    for c0 in range(0, 3 * fox_w, fox_w):
        y = _dot(xn, w_ref[:, c0:c0 + fox_w])
        if c0 == 0:
            y = y * q_scale
        fox_ref[:, c0:c0 + fox_w] = y.astype(BF16)
    base = 3 * fox_w
    cs = cs_ref[...]
    sn = sn_ref[...]
    for part in range(4):
        y = _dot(xn, w_ref[:, base + part * ret_w:base + (part + 1) * ret_w])
        if part < 2:
            for hh in range(RET_HEADS):
                sl = slice(hh * RET_HEAD_DIM, (hh + 1) * RET_HEAD_DIM)
                yh = y[:, sl]
                yh = yh * cs + pltpu.roll(yh, RET_HEAD_DIM // 2, axis=1) * sn
                if part == 1:
                    yh = yh * k_scale
                ret_ref[:, part * ret_w + sl.start:part * ret_w + sl.stop] = yh.astype(BF16)
        else:
            ret_ref[:, part * ret_w:(part + 1) * ret_w] = y.astype(BF16)
    z = _dot(xn, w_ref[:, base + 4 * ret_w:base + 4 * ret_w + LANES]) + bl_ref[...]
    logf_ref[...] = jnp.minimum(z, 0.0) - jnp.log(1.0 + jnp.exp(-jnp.abs(z)))


def _inproj_kernel(hr_ref, hm_ref, csr_ref, snr_ref, csm_ref, snm_ref, g_ref, w_ref, bl_ref,
                   foxr_ref, retr_ref, logfr_ref, foxm_ref, retm_ref, logfm_ref,
                   *, n_tiles, fox_w, ret_w):
    body = functools.partial(_inproj_body, fox_w=fox_w, ret_w=ret_w)
    t = pl.program_id(0)

    @pl.when(t < n_tiles)
    def _():
        body(hr_ref, csr_ref, snr_ref, g_ref, w_ref, bl_ref, foxr_ref, retr_ref, logfr_ref)

    @pl.when(t == n_tiles)
    def _():
        body(hm_ref, csm_ref, snm_ref, g_ref, w_ref, bl_ref, foxm_ref, retm_ref, logfm_ref)


def _inproj(h_r, h_m, cs_r, sn_r, cs_m, sn_m, g, w_all, bl, *, layer, seq, fox_w, ret_w):
    n_r, d = h_r.shape
    n_m = h_m.shape[0]
    tm = _row_tile(seq, INPROJ_ROWS)
    n_tiles = n_r // tm
    per_seq = seq // tm
    fox_cols, ret_cols = 3 * fox_w, 4 * ret_w

    def real(cols):
        return pl.BlockSpec((tm, cols), lambda t: (jnp.minimum(t, n_tiles - 1), 0))

    def table():
        return pl.BlockSpec((tm, LANES), lambda t: (jnp.minimum(t, n_tiles - 1) % per_seq, 0))

    def meta(cols):
        return pl.BlockSpec((n_m, cols), lambda t: (0, 0))

    kern = functools.partial(_inproj_kernel, n_tiles=n_tiles, fox_w=fox_w, ret_w=ret_w)
    return pl.pallas_call(
        kern,
        grid=(n_tiles + 1,),
        in_specs=[real(d), meta(d), table(), table(), meta(LANES), meta(LANES),
                  _const_spec((1, d)), _layer_spec(w_all.shape, layer), _const_spec((1, LANES))],
        out_specs=[real(fox_cols), real(ret_cols), real(LANES),
                   meta(fox_cols), meta(ret_cols), meta(LANES)],
        out_shape=[
            jax.ShapeDtypeStruct((n_r, fox_cols), BF16),
            jax.ShapeDtypeStruct((n_r, ret_cols), BF16),
            jax.ShapeDtypeStruct((n_r, LANES), F32),
            jax.ShapeDtypeStruct((n_m, fox_cols), BF16),
            jax.ShapeDtypeStruct((n_m, ret_cols), BF16),
            jax.ShapeDtypeStruct((n_m, LANES), F32),
        ],
        compiler_params=_params(1),
        name="inproj",
    )(h_r, h_m, cs_r, sn_r, cs_m, sn_m, g, w_all, bl)


def _bias_lane(pair, a):
    return (FOX_HEAD_DIM if a == 0 else 0) + BIAS_LANES * pair


def _split3(x):
    hi = x.astype(BF16)
    r1 = x - hi.astype(F32)
    mid = r1.astype(BF16)
    lo = (r1 - mid.astype(F32)).astype(BF16)
    return hi, mid, lo


def _bias_selector(n_heads):
    sel = np.zeros((N_SPLIT * LANES, LANES), np.float32)
    for head in range(n_heads):
        lane0 = _bias_lane(head // 2, head % 2)
        for p in range(N_SPLIT):
            sel[p * LANES + head, lane0 + p] = 1.0
            sel[p * LANES + head, lane0 + N_SPLIT + p] = -1.0
    return sel


def _decay_kernel(logf_ref, logfm_ref, sel_ref, qaug_ref, kaug_ref, qaugm_ref, kaugm_ref, c_scr,
                  *, n_heads):
    blk = LANES
    seq = logf_ref.shape[0]
    ri = lax.broadcasted_iota(jnp.int32, (blk, blk), 0)
    ci = lax.broadcasted_iota(jnp.int32, (blk, blk), 1)
    tri = (ci <= ri).astype(BF16)

    def prefix(x):
        hi, mid, lo = _split3(x)
        return _dot(tri, hi) + _dot(tri, mid) + _dot(tri, lo)

    cb = prefix(jnp.concatenate([logfm_ref[...], jnp.zeros((blk - N_META, LANES), F32)], axis=0))
    c_scr[0:blk, :] = cb
    carry = cb[N_META - 1:N_META, :]
    for r0 in range(0, seq, blk):
        cb = prefix(logf_ref[r0:r0 + blk, :]) + carry
        c_scr[blk + r0:2 * blk + r0, :] = cb
        carry = cb[blk - 1:blk, :]

    in_half = lax.broadcasted_iota(jnp.int32, (1, LANES), 1) % FOX_HEAD_DIM
    used = in_half < BIAS_LANES * (n_heads // 2)
    first = used & (in_half % BIAS_LANES < N_SPLIT)
    second = used & (in_half % BIAS_LANES >= N_SPLIT)

    def route(c, q_out, k_out):
        hi, mid, lo = _split3(c * LOG2E)
        routed = _dot(jnp.concatenate([hi, mid, lo], axis=1), sel_ref[...])
        q_out[...] = jnp.where(first, routed, jnp.where(second, 1.0, 0.0)).astype(BF16)
        k_out[...] = jnp.where(second, routed, jnp.where(first, 1.0, 0.0)).astype(BF16)

    route(c_scr[0:N_META, :], qaugm_ref, kaugm_ref)
    route(c_scr[blk:, :], qaug_ref, kaug_ref)


def _decay(logf_r, logf_m, sel, *, batch, n_heads):
    seq = logf_r.shape[0] // batch

    def rows(n):
        return pl.BlockSpec((None, n, LANES), lambda b: (b, 0, 0))

    return pl.pallas_call(
        functools.partial(_decay_kernel, n_heads=n_heads),
        grid=(batch,),
        in_specs=[rows(seq), rows(N_META), _const_spec(sel.shape)],
        out_specs=[rows(seq), rows(seq), rows(N_META), rows(N_META)],
        out_shape=[jax.ShapeDtypeStruct((batch, seq, LANES), BF16)] * 2
        + [jax.ShapeDtypeStruct((batch, N_META, LANES), BF16)] * 2,
        scratch_shapes=[pltpu.VMEM((LANES + seq, LANES), F32)],
        compiler_params=_params(1),
        name="decay",
    )(logf_r.reshape(batch, seq, LANES), logf_m.reshape(batch, N_META, LANES), sel)


def _fox_kernel(q_ref, k_ref, v_ref, qm_ref, km_ref, vm_ref, qaug_ref, kaug_ref, qaugm_ref, kaugm_ref,
                o_ref, om_ref, k_scr, km_scr, vt_scr, vmt_scr, vm_scr, s_scr, sm_scr, p_scr, pm_scr,
                *, tile, score_rows):
    seq = q_ref.shape[0]
    n_qblk = seq // tile
    pad_rows = LANES - N_META
    pair = pl.program_id(1)
    lane = lax.broadcasted_iota(jnp.int32, (1, LANES), 1)
    own = [lane < FOX_HEAD_DIM, lane >= FOX_HEAD_DIM]
    free0 = [FOX_HEAD_DIM, 0]
    n_slots = s_scr.shape[0]

    def augment(a, x, aug):
        lane0 = free0[a] + BIAS_LANES * pair
        mine = (lane >= lane0) & (lane < lane0 + BIAS_LANES)
        return jnp.where(own[a], x, jnp.where(mine, aug, jnp.zeros_like(aug)))

    pad = jnp.zeros((pad_rows, LANES), BF16)
    for a in (0, 1):
        ones_col = jnp.where(lane == free0[a], 1.0, 0.0).astype(BF16)
        km_scr[a] = jnp.concatenate([augment(a, km_ref[...], kaugm_ref[...]), pad], axis=0)
        v_meta = jnp.concatenate([jnp.where(own[a], vm_ref[...], ones_col), pad], axis=0)
        vm_scr[a] = v_meta
        vmt_scr[a] = v_meta.T
        for r0 in range(0, seq, tile):
            rows = slice(r0, r0 + tile)
            k_scr[a, rows, :] = augment(a, k_ref[rows, :], kaug_ref[rows, :])
            vt_scr[a, :, rows] = jnp.where(own[a], v_ref[rows, :], ones_col).T
    for slot in range(n_slots):
        pm_scr[slot, N_META:LANES, :] = jnp.zeros((pad_rows, tile), BF16)

    def meta_queries():
        row_m = lax.broadcasted_iota(jnp.int32, (N_META, LANES), 0)
        col_m = lax.broadcasted_iota(jnp.int32, (N_META, LANES), 1)
        outs = []
        for a in (0, 1):
            s = _dot_nt(augment(a, qm_ref[...], qaugm_ref[...]), km_scr[a])
            s = jnp.where(col_m <= row_m, s, -jnp.inf)
            p = jnp.exp2(s - jnp.max(s, axis=1, keepdims=True)).astype(BF16)
            acc = _dot(p, vm_scr[a])
            outs.append(acc * (1.0 / acc[:, free0[a]:free0[a] + 1]))
        om_ref[...] = jnp.where(own[0], outs[0], outs[1]).astype(om_ref.dtype)

    def score_stage(i, a):
        slot = (2 * i + a) % n_slots
        d0, kt = i * tile, (i + 1) * tile
        qh = augment(a, q_ref[d0:kt, :], qaug_ref[d0:kt, :])
        sm = _dot_nt(km_scr[a, 0:N_META, :], qh)
        sm_scr[slot] = sm
        mx = jnp.max(sm.reshape(N_META // 8, 8, tile), axis=0)
        for r0 in range(0, kt, score_rows):
            rows = min(score_rows, kt - r0)
            s = _dot_nt(k_scr[a, r0:r0 + rows, :], qh)
            if r0 + rows > d0:
                key_i = lax.broadcasted_iota(jnp.int32, (rows, tile), 0) + (r0 - d0)
                qry_i = lax.broadcasted_iota(jnp.int32, (rows, tile), 1)
                s = jnp.where(key_i <= qry_i, s, -jnp.inf)
            s_scr[slot, r0:r0 + rows, :] = s
            mx = jnp.maximum(mx, jnp.max(s.reshape(rows // 8, 8, tile), axis=0))
        return jnp.max(mx, axis=0, keepdims=True)

    def value_stage(i, maxes_i):
        kt = (i + 1) * tile
        slots = [(2 * i + a) % n_slots for a in (0, 1)]
        for a in (0, 1):
            pm_scr[slots[a], 0:N_META, :] = jnp.exp2(sm_scr[slots[a]] - maxes_i[a]).astype(BF16)
        for r0 in range(0, kt, LANES):
            for a in (0, 1):
                p_scr[slots[a], r0:r0 + LANES, :] = jnp.exp2(
                    s_scr[slots[a], r0:r0 + LANES, :] - maxes_i[a]).astype(BF16)
        outs = []
        for a in (0, 1):
            acc_t = (_dot(vt_scr[a, :, 0:kt], p_scr[slots[a], 0:kt, :])
                     + _dot(vmt_scr[a], pm_scr[slots[a]]))
            l = acc_t[free0[a]:free0[a] + 1, :]
            outs.append((acc_t * (1.0 / l)).T)
        return outs

    ahead = n_slots // 2 - 1
    maxes = {j: [score_stage(j, a) for a in (0, 1)] for j in range(min(ahead, n_qblk))}
    for i in range(n_qblk):
        if i + ahead < n_qblk:
            maxes[i + ahead] = [score_stage(i + ahead, a) for a in (0, 1)]
        outs = value_stage(i, maxes[i])
        o_ref[i * tile:(i + 1) * tile, :] = jnp.where(own[0], outs[0], outs[1]).astype(o_ref.dtype)
        if i == 0:
            meta_queries()


def _fox(fox_r, fox_m, qaug, kaug, qaug_m, kaug_m, *, batch):
    n_r, cols = fox_r.shape
    seq = n_r // batch
    width = cols // 3
    n_pairs = width // LANES
    tile = FOX_TILE
    assert seq % tile == 0 and BIAS_LANES * n_pairs <= FOX_HEAD_DIM
    qkv_r = fox_r.reshape(batch, seq, cols)
    qkv_m = fox_m.reshape(batch, N_META, cols)
    kern = functools.partial(_fox_kernel, tile=tile, score_rows=FOX_SCORE_ROWS)

    def cols_of(rows, j0):
        return pl.BlockSpec((None, rows, LANES), lambda b, p: (b, 0, j0 + p))

    def shared(rows):
        return pl.BlockSpec((None, rows, LANES), lambda b, p: (b, 0, 0))

    return pl.pallas_call(
        kern,
        grid=(batch, n_pairs),
        in_specs=[cols_of(seq, 0), cols_of(seq, n_pairs), cols_of(seq, 2 * n_pairs),
                  cols_of(N_META, 0), cols_of(N_META, n_pairs), cols_of(N_META, 2 * n_pairs),
                  shared(seq), shared(seq), shared(N_META), shared(N_META)],
        out_specs=[cols_of(seq, 0), cols_of(N_META, 0)],
        out_shape=[jax.ShapeDtypeStruct((batch, seq, width), BF16),
                   jax.ShapeDtypeStruct((batch, N_META, width), BF16)],
        scratch_shapes=[
            pltpu.VMEM((2, seq, LANES), BF16),
            pltpu.VMEM((2, LANES, LANES), BF16),
            pltpu.VMEM((2, LANES, seq), BF16),
            pltpu.VMEM((2, LANES, LANES), BF16),
            pltpu.VMEM((2, LANES, LANES), BF16),
            pltpu.VMEM((FOX_SLOTS, seq, tile), F32),
            pltpu.VMEM((FOX_SLOTS, N_META, tile), F32),
            pltpu.VMEM((FOX_SLOTS, seq, tile), BF16),
            pltpu.VMEM((FOX_SLOTS, LANES, tile), BF16),
        ],
        compiler_params=_params(2),
        name="fox",
    )(qkv_r, qkv_r, qkv_r, qkv_m, qkv_m, qkv_m, qaug, kaug, qaug_m, kaug_m)


def _ret_kernel(q_ref, k_ref, v_ref, g_ref, qm_ref, km_ref, vm_ref, gm_ref, gn_ref,
                o_ref, om_ref, state_scr, dec_scr):
    C = RET_CHUNK
    n_chunks = q_ref.shape[0] // C
    ri = lax.broadcasted_iota(jnp.int32, (C, C), 0)
    ci = lax.broadcasted_iota(jnp.int32, (C, C), 1)
    diff = ri - ci
    row_f = ri.astype(F32)

    for hh in range(RET_HEADS):
        log_g = RET_LOG_GAMMA[hh]
        dec_scr[hh, 0] = jnp.where(diff >= 0, jnp.exp(jnp.maximum(diff, 0).astype(F32) * log_g), 0.0)
        dec_scr[hh, 1] = jnp.exp((row_f + 1.0) * log_g)
        dec_scr[hh, 2] = jnp.exp((C - 1.0 - row_f) * log_g)
        dec_scr[hh, 3] = jnp.exp((N_META - 1.0 - row_f) * log_g)

    def head(hh):
        return slice(hh * RET_HEAD_DIM, (hh + 1) * RET_HEAD_DIM)

    def finish(o, g, out_ref, rows, hh):
        mu = jnp.mean(o, axis=-1, keepdims=True)
        d = o - mu
        var = jnp.mean(d * d, axis=-1, keepdims=True)
        y = d * lax.rsqrt(var + EPS) * gn_ref[:, head(hh)]
        g = g.astype(F32)
        out_ref[rows, head(hh)] = (g * (1.0 / (1.0 + jnp.exp(-g))) * y).astype(out_ref.dtype)

    def intra(q, k, v, hh):
        s = _dot_nt(q, k) * dec_scr[hh, 0]
        return _dot(s.astype(BF16), v)

    def decayed_keys(k, hh, table):
        return (k.astype(F32) * dec_scr[hh, table]).astype(BF16)

    pad = jnp.zeros((C - N_META, RET_HEAD_DIM), BF16)
    for hh in range(RET_HEADS):
        q, k, v = (jnp.concatenate([r[:, head(hh)], pad], axis=0) for r in (qm_ref, km_ref, vm_ref))
        finish(intra(q, k, v, hh)[0:N_META, :], gm_ref[:, head(hh)], om_ref, slice(0, N_META), hh)
        state_scr[hh] = _dot_tn(decayed_keys(k, hh, 3), v)

    def rows_of(c):
        return slice(c * C, (c + 1) * C)

    def scores(c):
        out = []
        for hh in range(RET_HEADS):
            q, k, v = (r[rows_of(c), head(hh)] for r in (q_ref, k_ref, v_ref))
            out.append((_dot_nt(q, k), _dot_tn(decayed_keys(k, hh, 2), v)))
        return out

    def carry(c, kv):
        out = []
        for hh in range(RET_HEADS):
            state = state_scr[hh]
            out.append(_dot(q_ref[rows_of(c), head(hh)], state.astype(BF16)) * dec_scr[hh, 1])
            chunk_dec = jnp.exp(jnp.full((1, 1), C * RET_LOG_GAMMA[hh], F32))
            state_scr[hh] = chunk_dec * state + kv[hh]
        return out

    def outputs(c, s, inter):
        for hh in range(RET_HEADS):
            o = _dot((s[hh] * dec_scr[hh, 0]).astype(BF16), v_ref[rows_of(c), head(hh)]) + inter[hh]
            finish(o, g_ref[rows_of(c), head(hh)], o_ref, rows_of(c), hh)

    cur = scores(0)
    inter = carry(0, [kv for _, kv in cur])
    for c in range(n_chunks):
        nxt = scores(c + 1) if c + 1 < n_chunks else None
        outputs(c, [s for s, _ in cur], inter)
        if nxt is not None:
            inter = carry(c + 1, [kv for _, kv in nxt])
        cur = nxt


def _retention(ret_r, ret_m, gn, *, batch):
    n_r, cols = ret_r.shape
    seq = n_r // batch
    width = cols // 4
    x_r = ret_r.reshape(batch, seq, cols)
    x_m = ret_m.reshape(batch, N_META, cols)

    def part(rows, j):
        return pl.BlockSpec((None, rows, width), lambda b: (b, 0, j))

    return pl.pallas_call(
        _ret_kernel,
        grid=(batch,),
        in_specs=[part(seq, j) for j in range(4)] + [part(N_META, j) for j in range(4)]
        + [_const_spec(gn.shape)],
        out_specs=[part(seq, 0), part(N_META, 0)],
        out_shape=[jax.ShapeDtypeStruct((batch, seq, width), BF16),
                   jax.ShapeDtypeStruct((batch, N_META, width), BF16)],
        scratch_shapes=[pltpu.VMEM((RET_HEADS, RET_HEAD_DIM, RET_HEAD_DIM), F32),
                        pltpu.VMEM((RET_HEADS, 4, RET_CHUNK, RET_HEAD_DIM), F32)],
        compiler_params=_params(1),
        name="retention",
    )(x_r, x_r, x_r, x_r, x_m, x_m, x_m, x_m, gn)


def _outffn_body(h_ref, fox_ref, ret_ref, wo_ref, g2_ref, wg_ref, wu_ref, wd_ref, gf_ref, o_ref,
                 hn_scr, act_scr, *, final):
    rows = h_ref.shape[0]
    fox_w = fox_ref.shape[1]
    h1 = h_ref[...] + _dot(fox_ref[...], wo_ref[0:fox_w, :]) + _dot(ret_ref[...], wo_ref[fox_w:, :])
    hn_scr[0:rows, :] = _rmsnorm(h1, g2_ref[...]).astype(BF16)
    o_ref[...] = h1
    for c0 in range(0, wg_ref.shape[1], FF_CHUNK):
        hn = hn_scr[0:rows, :]
        gate = _dot(hn, wg_ref[:, c0:c0 + FF_CHUNK])
        up = _dot(hn, wu_ref[:, c0:c0 + FF_CHUNK])
        act_scr[0:rows, c0:c0 + FF_CHUNK] = (gate * (1.0 / (1.0 + jnp.exp(-gate))) * up).astype(BF16)
    out = o_ref[...] + _dot(act_scr[0:rows, :], wd_ref[...])
    if final:
        out = _rmsnorm(out, gf_ref[...])
    o_ref[...] = out


def _outffn_kernel(*refs, n_tiles, final):
    if final:
        (hr, foxr, retr, wo, g2, wg, wu, wd, gf, outr, hn_scr, act_scr) = refs
    else:
        (hr, foxr, retr, hm, foxm, retm, wo, g2, wg, wu, wd, gf, outr, outm, hn_scr, act_scr) = refs
    body = functools.partial(_outffn_body, final=final)
    t = pl.program_id(0)

    @pl.when(t < n_tiles)
    def _():
        body(hr, foxr, retr, wo, g2, wg, wu, wd, gf, outr, hn_scr, act_scr)

    if not final:
        @pl.when(t == n_tiles)
        def _():
            body(hm, foxm, retm, wo, g2, wg, wu, wd, gf, outm, hn_scr, act_scr)


def _outffn(h_r, fox_r, ret_r, h_m, fox_m, ret_m, wo, g2, wg, wu, wd, gf, *, layer, final):
    n_r, d = h_r.shape
    n_m = h_m.shape[0]
    d_ff = wg.shape[2]
    tm = _row_tile(n_r, OUTFFN_ROWS)
    n_tiles = n_r // tm
    assert n_m <= tm

    def real(cols):
        return pl.BlockSpec((tm, cols), lambda t: (jnp.minimum(t, n_tiles - 1), 0))

    def meta(cols):
        return pl.BlockSpec((n_m, cols), lambda t: (0, 0))

    weights = [_layer_spec(wo.shape, layer), _const_spec(g2.shape), _layer_spec(wg.shape, layer),
               _layer_spec(wu.shape, layer), _layer_spec(wd.shape, layer), _const_spec(gf.shape)]
    real_in = [real(d), real(fox_r.shape[1]), real(ret_r.shape[1])]
    kern = functools.partial(_outffn_kernel, n_tiles=n_tiles, final=final)
    scratch = [pltpu.VMEM((tm, d), BF16),
               pltpu.VMEM((tm, d_ff), BF16)]
    if final:
        return pl.pallas_call(
            kern, grid=(n_tiles,), in_specs=real_in + weights, out_specs=real(d),
            out_shape=jax.ShapeDtypeStruct((n_r, d), F32), scratch_shapes=scratch,
            compiler_params=_params(1), name="outffn_last",
        )(h_r, fox_r, ret_r, wo, g2, wg, wu, wd, gf), None
    meta_in = [meta(d), meta(fox_m.shape[1]), meta(ret_m.shape[1])]
    return pl.pallas_call(
        kern, grid=(n_tiles + 1,), in_specs=real_in + meta_in + weights,
        out_specs=[real(d), meta(d)],
        out_shape=[jax.ShapeDtypeStruct((n_r, d), F32), jax.ShapeDtypeStruct((n_m, d), F32)],
        scratch_shapes=scratch, compiler_params=_params(1), name="outffn",
    )(h_r, fox_r, ret_r, h_m, fox_m, ret_m, wo, g2, wg, wu, wd, gf)


def kernel(x, meta_tokens, attn_norm, w_in, b_fgate, ret_gn, w_out, ffn_norm,
           w_gate, w_up, w_down, final_norm):
    batch, seq, d = x.shape
    depth = w_in.shape[0]
    fox_heads = b_fgate.shape[1]
    fox_w = fox_heads * FOX_HEAD_DIM
    ret_w = RET_HEADS * RET_HEAD_DIM
    d_ff = w_gate.shape[2]
    assert w_in.shape[2] == 3 * fox_w + fox_heads + 4 * ret_w
    assert fox_w % LANES == 0 and fox_heads <= LANES and d_ff % FF_CHUNK == 0
    assert seq % FOX_TILE == 0 and seq % RET_CHUNK == 0

    h_r = x.reshape(batch * seq, d)
    h_m = jnp.broadcast_to(meta_tokens[None].astype(x.dtype), (batch, N_META, d)).reshape(batch * N_META, d)

    inv_freq = ROPE_BASE ** (-jnp.arange(0, RET_HEAD_DIM, 2, dtype=F32) / RET_HEAD_DIM)
    ang = jnp.arange(seq + N_META, dtype=F32)[:, None] * inv_freq[None, :]
    cos, sin = jnp.cos(ang), jnp.sin(ang)
    cs = jnp.concatenate([cos, cos], axis=1)
    sn = jnp.concatenate([-sin, sin], axis=1)
    cs_r, sn_r = cs[N_META:], sn[N_META:]
    cs_m, sn_m = jnp.tile(cs[:N_META], (batch, 1)), jnp.tile(sn[:N_META], (batch, 1))

    split = 3 * fox_w
    w_all = jnp.concatenate(
        [w_in[..., :split], w_in[..., split + fox_heads:], w_in[..., split:split + fox_heads],
         jnp.zeros((depth, d, LANES - fox_heads), w_in.dtype)], axis=-1).astype(BF16)
    wo, wg, wu, wd = (w.astype(BF16) for w in (w_out, w_gate, w_up, w_down))
    sel = jnp.asarray(_bias_selector(fox_heads), BF16)
    gf = final_norm.reshape(1, d)

    for i in range(depth):
        bl = jnp.pad(b_fgate[i], (0, LANES - fox_heads)).reshape(1, LANES)
        fox_r, ret_r, logf_r, fox_m, ret_m, logf_m = _inproj(
            h_r, h_m, cs_r, sn_r, cs_m, sn_m, attn_norm[i].reshape(1, d), w_all, bl,
            layer=i, seq=seq, fox_w=fox_w, ret_w=ret_w)
        qaug, kaug, qaug_m, kaug_m = _decay(logf_r, logf_m, sel, batch=batch, n_heads=fox_heads)
        fo_r, fo_m = _fox(fox_r, fox_m, qaug, kaug, qaug_m, kaug_m, batch=batch)
        ro_r, ro_m = _retention(ret_r, ret_m, ret_gn[i].reshape(1, ret_w), batch=batch)
        h_r, h_m = _outffn(h_r, fo_r.reshape(batch * seq, fox_w), ro_r.reshape(batch * seq, ret_w),
                           h_m, fo_m.reshape(batch * N_META, fox_w), ro_m.reshape(batch * N_META, ret_w),
                           wo, ffn_norm[i].reshape(1, d), wg, wu, wd, gf,
                           layer=i, final=(i == depth - 1))
    return h_r.reshape(batch, seq, d)
```

```python
import functools
import math

import numpy as np
import jax
import jax.numpy as jnp
from jax import lax
from jax.experimental import pallas as pl
from jax.experimental.pallas import tpu as pltpu

N_META = 16
FOX_HEAD_DIM = 64
RET_HEADS = 4
RET_HEAD_DIM = 128
RET_CHUNK = 128
ROPE_BASE = 10000.0
EPS = 1e-6
RET_LOG_GAMMA = tuple(math.log(1.0 - 2.0 ** (-5 - h)) for h in range(RET_HEADS))
LOG2E = math.log2(math.e)

LANES = 128
BF16_SUBLANES = 16
VMEM_LIMIT_BYTES = 56 * 1024 * 1024

N_SPLIT = 3
BIAS_LANES = 2 * N_SPLIT

INPROJ_ROWS = 1024
OUTFFN_ROWS = 512
FF_CHUNK = 256
FOX_TILE = 256
FOX_SCORE_ROWS = 128
FOX_SLOTS = 6

F32 = jnp.float32
BF16 = jnp.bfloat16

assert RET_CHUNK == RET_HEAD_DIM == LANES


def _dot(a, b):
    return jnp.dot(a, b, preferred_element_type=F32)


def _dot_nt(a, b):
    return lax.dot_general(a, b, (((1,), (1,)), ((), ())), preferred_element_type=F32)


def _dot_tn(a, b):
    return lax.dot_general(a, b, (((0,), (0,)), ((), ())), preferred_element_type=F32)


def _rmsnorm(x, g):
    return x * lax.rsqrt(jnp.mean(x * x, axis=-1, keepdims=True) + EPS) * g


def _row_tile(n, cap):
    best = None
    for t in range(BF16_SUBLANES, min(n, cap) + 1, BF16_SUBLANES):
        if n % t == 0:
            best = t
    assert best is not None, n
    return best


def _const_spec(shape):
    nd = len(shape)
    return pl.BlockSpec(shape, lambda *_: (0,) * nd, pipeline_mode=pl.Buffered(1))


def _layer_spec(shape, layer):
    nd = len(shape) - 1
    return pl.BlockSpec((None,) + tuple(shape[1:]), lambda *_: (layer,) + (0,) * nd,
                        pipeline_mode=pl.Buffered(1))


def _params(n_axes):
    return pltpu.CompilerParams(dimension_semantics=("arbitrary",) * n_axes,
                                vmem_limit_bytes=VMEM_LIMIT_BYTES)


def _inproj_body(h_ref, cs_ref, sn_ref, g_ref, w_ref, bl_ref, fox_ref, ret_ref, logf_ref,
                 *, fox_w, ret_w):
    xn = _rmsnorm(h_ref[...], g_ref[...]).astype(BF16)
    q_scale = FOX_HEAD_DIM ** -0.5 * LOG2E
    k_scale = RET_HEAD_DIM ** -0.5
    for c0 in range(0, 3 * fox_w, fox_w):
        y = _dot(xn, w_ref[:, c0:c0 + fox_w])
        if c0 == 0:
            y = y * q_scale
        fox_ref[:, c0:c0 + fox_w] = y.astype(BF16)
    base = 3 * fox_w
    cs = cs_ref[...]
    sn = sn_ref[...]
    for part in range(4):
        y = _dot(xn, w_ref[:, base + part * ret_w:base + (part + 1) * ret_w])
        if part < 2:
            for hh in range(RET_HEADS):
                sl = slice(hh * RET_HEAD_DIM, (hh + 1) * RET_HEAD_DIM)
                yh = y[:, sl]
                yh = yh * cs + pltpu.roll(yh, RET_HEAD_DIM // 2, axis=1) * sn
                if part == 1:
                    yh = yh * k_scale
                ret_ref[:, part * ret_w + sl.start:part * ret_w + sl.stop] = yh.astype(BF16)
        else:
            ret_ref[:, part * ret_w:(part + 1) * ret_w] = y.astype(BF16)
    z = _dot(xn, w_ref[:, base + 4 * ret_w:base + 4 * ret_w + LANES]) + bl_ref[...]
    logf_ref[...] = jnp.minimum(z, 0.0) - jnp.log(1.0 + jnp.exp(-jnp.abs(z)))


def _inproj_kernel(hr_ref, hm_ref, csr_ref, snr_ref, csm_ref, snm_ref, g_ref, w_ref, bl_ref,
                   foxr_ref, retr_ref, logfr_ref, foxm_ref, retm_ref, logfm_ref,
                   *, n_tiles, fox_w, ret_w):
    body = functools.partial(_inproj_body, fox_w=fox_w, ret_w=ret_w)
    t = pl.program_id(0)

    @pl.when(t < n_tiles)
    def _():
        body(hr_ref, csr_ref, snr_ref, g_ref, w_ref, bl_ref, foxr_ref, retr_ref, logfr_ref)

    @pl.when(t == n_tiles)
    def _():
        body(hm_ref, csm_ref, snm_ref, g_ref, w_ref, bl_ref, foxm_ref, retm_ref, logfm_ref)


def _inproj(h_r, h_m, cs_r, sn_r, cs_m, sn_m, g, w_all, bl, *, layer, seq, fox_w, ret_w):
    n_r, d = h_r.shape
    n_m = h_m.shape[0]
    tm = _row_tile(seq, INPROJ_ROWS)
    n_tiles = n_r // tm
    per_seq = seq // tm
    fox_cols, ret_cols = 3 * fox_w, 4 * ret_w

    def real(cols):
        return pl.BlockSpec((tm, cols), lambda t: (jnp.minimum(t, n_tiles - 1), 0))

    def table():
        return pl.BlockSpec((tm, LANES), lambda t: (jnp.minimum(t, n_tiles - 1) % per_seq, 0))

    def meta(cols):
        return pl.BlockSpec((n_m, cols), lambda t: (0, 0))

    kern = functools.partial(_inproj_kernel, n_tiles=n_tiles, fox_w=fox_w, ret_w=ret_w)
    return pl.pallas_call(
        kern,
        grid=(n_tiles + 1,),
        in_specs=[real(d), meta(d), table(), table(), meta(LANES), meta(LANES),
                  _const_spec((1, d)), _layer_spec(w_all.shape, layer), _const_spec((1, LANES))],
        out_specs=[real(fox_cols), real(ret_cols), real(LANES),
                   meta(fox_cols), meta(ret_cols), meta(LANES)],
        out_shape=[
            jax.ShapeDtypeStruct((n_r, fox_cols), BF16),
            jax.ShapeDtypeStruct((n_r, ret_cols), BF16),
            jax.ShapeDtypeStruct((n_r, LANES), F32),
            jax.ShapeDtypeStruct((n_m, fox_cols), BF16),
            jax.ShapeDtypeStruct((n_m, ret_cols), BF16),
            jax.ShapeDtypeStruct((n_m, LANES), F32),
        ],
        compiler_params=_params(1),
        name="inproj",
    )(h_r, h_m, cs_r, sn_r, cs_m, sn_m, g, w_all, bl)


def _bias_lane(pair, a):
    return (FOX_HEAD_DIM if a == 0 else 0) + BIAS_LANES * pair


def _split3(x):
    hi = x.astype(BF16)
    r1 = x - hi.astype(F32)
    mid = r1.astype(BF16)
    lo = (r1 - mid.astype(F32)).astype(BF16)
    return hi, mid, lo


def _bias_selector(n_heads):
    sel = np.zeros((N_SPLIT * LANES, LANES), np.float32)
    for head in range(n_heads):
        lane0 = _bias_lane(head // 2, head % 2)
        for p in range(N_SPLIT):
            sel[p * LANES + head, lane0 + p] = 1.0
            sel[p * LANES + head, lane0 + N_SPLIT + p] = -1.0
    return sel


def _decay_kernel(logf_ref, logfm_ref, sel_ref, qaug_ref, kaug_ref, qaugm_ref, kaugm_ref, c_scr,
                  *, n_heads):
    blk = LANES
    seq = logf_ref.shape[0]
    ri = lax.broadcasted_iota(jnp.int32, (blk, blk), 0)
    ci = lax.broadcasted_iota(jnp.int32, (blk, blk), 1)
    tri = (ci <= ri).astype(BF16)

    def prefix(x):
        hi, mid, lo = _split3(x)
        return _dot(tri, hi) + _dot(tri, mid) + _dot(tri, lo)

    cb = prefix(jnp.concatenate([logfm_ref[...], jnp.zeros((blk - N_META, LANES), F32)], axis=0))
    c_scr[0:blk, :] = cb
    carry = cb[N_META - 1:N_META, :]
    for r0 in range(0, seq, blk):
        cb = prefix(logf_ref[r0:r0 + blk, :]) + carry
        c_scr[blk + r0:2 * blk + r0, :] = cb
        carry = cb[blk - 1:blk, :]

    in_half = lax.broadcasted_iota(jnp.int32, (1, LANES), 1) % FOX_HEAD_DIM
    used = in_half < BIAS_LANES * (n_heads // 2)
    first = used & (in_half % BIAS_LANES < N_SPLIT)
    second = used & (in_half % BIAS_LANES >= N_SPLIT)

    def route(c, q_out, k_out):
        hi, mid, lo = _split3(c * LOG2E)
        routed = _dot(jnp.concatenate([hi, mid, lo], axis=1), sel_ref[...])
        q_out[...] = jnp.where(first, routed, jnp.where(second, 1.0, 0.0)).astype(BF16)
        k_out[...] = jnp.where(second, routed, jnp.where(first, 1.0, 0.0)).astype(BF16)

    route(c_scr[0:N_META, :], qaugm_ref, kaugm_ref)
    route(c_scr[blk:, :], qaug_ref, kaug_ref)


def _decay(logf_r, logf_m, sel, *, batch, n_heads):
    seq = logf_r.shape[0] // batch

    def rows(n):
        return pl.BlockSpec((None, n, LANES), lambda b: (b, 0, 0))

    return pl.pallas_call(
        functools.partial(_decay_kernel, n_heads=n_heads),
        grid=(batch,),
        in_specs=[rows(seq), rows(N_META), _const_spec(sel.shape)],
        out_specs=[rows(seq), rows(seq), rows(N_META), rows(N_META)],
        out_shape=[jax.ShapeDtypeStruct((batch, seq, LANES), BF16)] * 2
        + [jax.ShapeDtypeStruct((batch, N_META, LANES), BF16)] * 2,
        scratch_shapes=[pltpu.VMEM((LANES + seq, LANES), F32)],
        compiler_params=_params(1),
        name="decay",
    )(logf_r.reshape(batch, seq, LANES), logf_m.reshape(batch, N_META, LANES), sel)


def _fox_kernel(q_ref, k_ref, v_ref, qm_ref, km_ref, vm_ref, qaug_ref, kaug_ref, qaugm_ref, kaugm_ref,
                o_ref, om_ref, k_scr, km_scr, vt_scr, vmt_scr, vm_scr, s_scr, sm_scr, p_scr, pm_scr, qt_scr,
                *, tile, score_rows):
    seq = q_ref.shape[0]
    n_qblk = seq // tile
    pad_rows = LANES - N_META
    pair = pl.program_id(1)
    lane = lax.broadcasted_iota(jnp.int32, (1, LANES), 1)
    own = [lane < FOX_HEAD_DIM, lane >= FOX_HEAD_DIM]
    free0 = [FOX_HEAD_DIM, 0]
    n_slots = s_scr.shape[0]

    def augment(a, x, aug):
        lane0 = free0[a] + BIAS_LANES * pair
        mine = (lane >= lane0) & (lane < lane0 + BIAS_LANES)
        return jnp.where(own[a], x, jnp.where(mine, aug, jnp.zeros_like(aug)))

    pad = jnp.zeros((pad_rows, LANES), BF16)
    for a in (0, 1):
        ones_col = jnp.where(lane == free0[a], 1.0, 0.0).astype(BF16)
        km_scr[a] = jnp.concatenate([augment(a, km_ref[...], kaugm_ref[...]), pad], axis=0)
        v_meta = jnp.concatenate([jnp.where(own[a], vm_ref[...], ones_col), pad], axis=0)
        vm_scr[a] = v_meta
        vmt_scr[a] = v_meta.T
        for r0 in range(0, seq, tile):
            rows = slice(r0, r0 + tile)
            k_scr[a, rows, :] = augment(a, k_ref[rows, :], kaug_ref[rows, :])
            vt_scr[a, :, rows] = jnp.where(own[a], v_ref[rows, :], ones_col).T
    for slot in range(n_slots):
        pm_scr[slot, N_META:LANES, :] = jnp.zeros((pad_rows, tile), BF16)

    def meta_queries():
        row_m = lax.broadcasted_iota(jnp.int32, (N_META, LANES), 0)
        col_m = lax.broadcasted_iota(jnp.int32, (N_META, LANES), 1)
        outs = []
        for a in (0, 1):
            s = _dot_nt(augment(a, qm_ref[...], qaugm_ref[...]), km_scr[a])
            s = jnp.where(col_m <= row_m, s, -jnp.inf)
            p = jnp.exp2(s - jnp.max(s, axis=1, keepdims=True)).astype(BF16)
            acc = _dot(p, vm_scr[a])
            outs.append(acc * (1.0 / acc[:, free0[a]:free0[a] + 1]))
        om_ref[...] = jnp.where(own[0], outs[0], outs[1]).astype(om_ref.dtype)

    def score_stage(i, a):
        slot = (2 * i + a) % n_slots
        d0, kt = i * tile, (i + 1) * tile
        qt_scr[slot] = augment(a, q_ref[d0:kt, :], qaug_ref[d0:kt, :]).T
        qt = qt_scr[slot]
        sm = _dot(km_scr[a, 0:N_META, :], qt)
        sm_scr[slot] = sm
        mx = jnp.max(sm.reshape(N_META // 8, 8, tile), axis=0)
        for r0 in range(0, kt, score_rows):
            rows = min(score_rows, kt - r0)
            s = _dot(k_scr[a, r0:r0 + rows, :], qt)
            if r0 + rows > d0:
                key_i = lax.broadcasted_iota(jnp.int32, (rows, tile), 0) + (r0 - d0)
                qry_i = lax.broadcasted_iota(jnp.int32, (rows, tile), 1)
                s = jnp.where(key_i <= qry_i, s, -jnp.inf)
            s_scr[slot, r0:r0 + rows, :] = s
            mx = jnp.maximum(mx, jnp.max(s.reshape(rows // 8, 8, tile), axis=0))
        return jnp.max(mx, axis=0, keepdims=True)

    def value_stage(i, maxes_i):
        kt = (i + 1) * tile
        slots = [(2 * i + a) % n_slots for a in (0, 1)]
        for a in (0, 1):
            pm_scr[slots[a], 0:N_META, :] = jnp.exp2(sm_scr[slots[a]] - maxes_i[a]).astype(BF16)
        for r0 in range(0, kt, LANES):
            for a in (0, 1):
                p_scr[slots[a], r0:r0 + LANES, :] = jnp.exp2(
                    s_scr[slots[a], r0:r0 + LANES, :] - maxes_i[a]).astype(BF16)
        outs = []
        for a in (0, 1):
            acc_t = (_dot(vt_scr[a, :, 0:kt], p_scr[slots[a], 0:kt, :])
                     + _dot(vmt_scr[a], pm_scr[slots[a]]))
            l = acc_t[free0[a]:free0[a] + 1, :]
            outs.append((acc_t * (1.0 / l)).T)
        return outs

    ahead = n_slots // 2 - 1
    maxes = {j: [score_stage(j, a) for a in (0, 1)] for j in range(min(ahead, n_qblk))}
    for i in range(n_qblk):
        if i + ahead < n_qblk:
            maxes[i + ahead] = [score_stage(i + ahead, a) for a in (0, 1)]
        outs = value_stage(i, maxes[i])
        o_ref[i * tile:(i + 1) * tile, :] = jnp.where(own[0], outs[0], outs[1]).astype(o_ref.dtype)
        if i == 0:
            meta_queries()


def _fox(fox_r, fox_m, qaug, kaug, qaug_m, kaug_m, *, batch):
    n_r, cols = fox_r.shape
    seq = n_r // batch
    width = cols // 3
    n_pairs = width // LANES
    tile = FOX_TILE
    assert seq % tile == 0 and BIAS_LANES * n_pairs <= FOX_HEAD_DIM
    qkv_r = fox_r.reshape(batch, seq, cols)
    qkv_m = fox_m.reshape(batch, N_META, cols)
    kern = functools.partial(_fox_kernel, tile=tile, score_rows=FOX_SCORE_ROWS)

    def cols_of(rows, j0):
        return pl.BlockSpec((None, rows, LANES), lambda b, p: (b, 0, j0 + p))

    def shared(rows):
        return pl.BlockSpec((None, rows, LANES), lambda b, p: (b, 0, 0))

    return pl.pallas_call(
        kern,
        grid=(batch, n_pairs),
        in_specs=[cols_of(seq, 0), cols_of(seq, n_pairs), cols_of(seq, 2 * n_pairs),
                  cols_of(N_META, 0), cols_of(N_META, n_pairs), cols_of(N_META, 2 * n_pairs),
                  shared(seq), shared(seq), shared(N_META), shared(N_META)],
        out_specs=[cols_of(seq, 0), cols_of(N_META, 0)],
        out_shape=[jax.ShapeDtypeStruct((batch, seq, width), BF16),
                   jax.ShapeDtypeStruct((batch, N_META, width), BF16)],
        scratch_shapes=[
            pltpu.VMEM((2, seq, LANES), BF16),
            pltpu.VMEM((2, LANES, LANES), BF16),
            pltpu.VMEM((2, LANES, seq), BF16),
            pltpu.VMEM((2, LANES, LANES), BF16),
            pltpu.VMEM((2, LANES, LANES), BF16),
            pltpu.VMEM((FOX_SLOTS, seq, tile), F32),
            pltpu.VMEM((FOX_SLOTS, N_META, tile), F32),
            pltpu.VMEM((FOX_SLOTS, seq, tile), BF16),
            pltpu.VMEM((FOX_SLOTS, LANES, tile), BF16),
            pltpu.VMEM((FOX_SLOTS, LANES, tile), BF16),
        ],
        compiler_params=_params(2),
        name="fox",
    )(qkv_r, qkv_r, qkv_r, qkv_m, qkv_m, qkv_m, qaug, kaug, qaug_m, kaug_m)


def _ret_kernel(q_ref, k_ref, v_ref, g_ref, qm_ref, km_ref, vm_ref, gm_ref, gn_ref,
                o_ref, om_ref, state_scr, dec_scr):
    C = RET_CHUNK
    n_chunks = q_ref.shape[0] // C
    ri = lax.broadcasted_iota(jnp.int32, (C, C), 0)
    ci = lax.broadcasted_iota(jnp.int32, (C, C), 1)
    diff = ri - ci
    row_f = ri.astype(F32)

    for hh in range(RET_HEADS):
        log_g = RET_LOG_GAMMA[hh]
        dec_scr[hh, 0] = jnp.where(diff >= 0, jnp.exp(jnp.maximum(diff, 0).astype(F32) * log_g), 0.0)
        dec_scr[hh, 1] = jnp.exp((row_f + 1.0) * log_g)
        dec_scr[hh, 2] = jnp.exp((C - 1.0 - row_f) * log_g)
        dec_scr[hh, 3] = jnp.exp((N_META - 1.0 - row_f) * log_g)

    def head(hh):
        return slice(hh * RET_HEAD_DIM, (hh + 1) * RET_HEAD_DIM)

    def finish(o, g, out_ref, rows, hh):
        mu = jnp.mean(o, axis=-1, keepdims=True)
        d = o - mu
        var = jnp.mean(d * d, axis=-1, keepdims=True)
        y = d * lax.rsqrt(var + EPS) * gn_ref[:, head(hh)]
        g = g.astype(F32)
        out_ref[rows, head(hh)] = (g * (1.0 / (1.0 + jnp.exp(-g))) * y).astype(out_ref.dtype)

    def intra(q, k, v, hh):
        s = _dot_nt(q, k) * dec_scr[hh, 0]
        return _dot(s.astype(BF16), v)

    def decayed_keys(k, hh, table):
        return (k.astype(F32) * dec_scr[hh, table]).astype(BF16)

    pad = jnp.zeros((C - N_META, RET_HEAD_DIM), BF16)
    for hh in range(RET_HEADS):
        q, k, v = (jnp.concatenate([r[:, head(hh)], pad], axis=0) for r in (qm_ref, km_ref, vm_ref))
        finish(intra(q, k, v, hh)[0:N_META, :], gm_ref[:, head(hh)], om_ref, slice(0, N_META), hh)
        state_scr[hh] = _dot_tn(decayed_keys(k, hh, 3), v)

    def rows_of(c):
        return slice(c * C, (c + 1) * C)

    def scores(c):
        out = []
        for hh in range(RET_HEADS):
            q, k, v = (r[rows_of(c), head(hh)] for r in (q_ref, k_ref, v_ref))
            out.append((_dot_nt(q, k), _dot_tn(decayed_keys(k, hh, 2), v)))
        return out

    def carry(c, kv):
        out = []
        for hh in range(RET_HEADS):
            state = state_scr[hh]
            out.append(_dot(q_ref[rows_of(c), head(hh)], state.astype(BF16)) * dec_scr[hh, 1])
            chunk_dec = jnp.exp(jnp.full((1, 1), C * RET_LOG_GAMMA[hh], F32))
            state_scr[hh] = chunk_dec * state + kv[hh]
        return out

    def outputs(c, s, inter):
        for hh in range(RET_HEADS):
            o = _dot((s[hh] * dec_scr[hh, 0]).astype(BF16), v_ref[rows_of(c), head(hh)]) + inter[hh]
            finish(o, g_ref[rows_of(c), head(hh)], o_ref, rows_of(c), hh)

    cur = scores(0)
    inter = carry(0, [kv for _, kv in cur])
    for c in range(n_chunks):
        nxt = scores(c + 1) if c + 1 < n_chunks else None
        outputs(c, [s for s, _ in cur], inter)
        if nxt is not None:
            inter = carry(c + 1, [kv for _, kv in nxt])
        cur = nxt


def _retention(ret_r, ret_m, gn, *, batch):
    n_r, cols = ret_r.shape
    seq = n_r // batch
    width = cols // 4
    x_r = ret_r.reshape(batch, seq, cols)
    x_m = ret_m.reshape(batch, N_META, cols)

    def part(rows, j):
        return pl.BlockSpec((None, rows, width), lambda b: (b, 0, j))

    return pl.pallas_call(
        _ret_kernel,
        grid=(batch,),
        in_specs=[part(seq, j) for j in range(4)] + [part(N_META, j) for j in range(4)]
        + [_const_spec(gn.shape)],
        out_specs=[part(seq, 0), part(N_META, 0)],
        out_shape=[jax.ShapeDtypeStruct((batch, seq, width), BF16),
                   jax.ShapeDtypeStruct((batch, N_META, width), BF16)],
        scratch_shapes=[pltpu.VMEM((RET_HEADS, RET_HEAD_DIM, RET_HEAD_DIM), F32),
                        pltpu.VMEM((RET_HEADS, 4, RET_CHUNK, RET_HEAD_DIM), F32)],
        compiler_params=_params(1),
        name="retention",
    )(x_r, x_r, x_r, x_r, x_m, x_m, x_m, x_m, gn)


def _outffn_body(h_ref, fox_ref, ret_ref, wo_ref, g2_ref, wg_ref, wu_ref, wd_ref, gf_ref, o_ref,
                 hn_scr, act_scr, *, final):
    rows = h_ref.shape[0]
    fox_w = fox_ref.shape[1]
    h1 = h_ref[...] + _dot(fox_ref[...], wo_ref[0:fox_w, :]) + _dot(ret_ref[...], wo_ref[fox_w:, :])
    hn_scr[0:rows, :] = _rmsnorm(h1, g2_ref[...]).astype(BF16)
    o_ref[...] = h1
    for c0 in range(0, wg_ref.shape[1], FF_CHUNK):
        hn = hn_scr[0:rows, :]
        gate = _dot(hn, wg_ref[:, c0:c0 + FF_CHUNK])
        up = _dot(hn, wu_ref[:, c0:c0 + FF_CHUNK])
        act_scr[0:rows, c0:c0 + FF_CHUNK] = (gate * (1.0 / (1.0 + jnp.exp(-gate))) * up).astype(BF16)
    out = o_ref[...] + _dot(act_scr[0:rows, :], wd_ref[...])
    if final:
        out = _rmsnorm(out, gf_ref[...])
    o_ref[...] = out


def _outffn_kernel(*refs, n_tiles, final):
    if final:
        (hr, foxr, retr, wo, g2, wg, wu, wd, gf, outr, hn_scr, act_scr) = refs
    else:
        (hr, foxr, retr, hm, foxm, retm, wo, g2, wg, wu, wd, gf, outr, outm, hn_scr, act_scr) = refs
    body = functools.partial(_outffn_body, final=final)
    t = pl.program_id(0)

    @pl.when(t < n_tiles)
    def _():
        body(hr, foxr, retr, wo, g2, wg, wu, wd, gf, outr, hn_scr, act_scr)

    if not final:
        @pl.when(t == n_tiles)
        def _():
            body(hm, foxm, retm, wo, g2, wg, wu, wd, gf, outm, hn_scr, act_scr)


def _outffn(h_r, fox_r, ret_r, h_m, fox_m, ret_m, wo, g2, wg, wu, wd, gf, *, layer, final):
    n_r, d = h_r.shape
    n_m = h_m.shape[0]
    d_ff = wg.shape[2]
    tm = _row_tile(n_r, OUTFFN_ROWS)
    n_tiles = n_r // tm
    assert n_m <= tm

    def real(cols):
        return pl.BlockSpec((tm, cols), lambda t: (jnp.minimum(t, n_tiles - 1), 0))

    def meta(cols):
        return pl.BlockSpec((n_m, cols), lambda t: (0, 0))

    weights = [_layer_spec(wo.shape, layer), _const_spec(g2.shape), _layer_spec(wg.shape, layer),
               _layer_spec(wu.shape, layer), _layer_spec(wd.shape, layer), _const_spec(gf.shape)]
    real_in = [real(d), real(fox_r.shape[1]), real(ret_r.shape[1])]
    kern = functools.partial(_outffn_kernel, n_tiles=n_tiles, final=final)
    scratch = [pltpu.VMEM((tm, d), BF16),
               pltpu.VMEM((tm, d_ff), BF16)]
    if final:
        return pl.pallas_call(
            kern, grid=(n_tiles,), in_specs=real_in + weights, out_specs=real(d),
            out_shape=jax.ShapeDtypeStruct((n_r, d), F32), scratch_shapes=scratch,
            compiler_params=_params(1), name="outffn_last",
        )(h_r, fox_r, ret_r, wo, g2, wg, wu, wd, gf), None
    meta_in = [meta(d), meta(fox_m.shape[1]), meta(ret_m.shape[1])]
    return pl.pallas_call(
        kern, grid=(n_tiles + 1,), in_specs=real_in + meta_in + weights,
        out_specs=[real(d), meta(d)],
        out_shape=[jax.ShapeDtypeStruct((n_r, d), F32), jax.ShapeDtypeStruct((n_m, d), F32)],
        scratch_shapes=scratch, compiler_params=_params(1), name="outffn",
    )(h_r, fox_r, ret_r, h_m, fox_m, ret_m, wo, g2, wg, wu, wd, gf)


def kernel(x, meta_tokens, attn_norm, w_in, b_fgate, ret_gn, w_out, ffn_norm,
           w_gate, w_up, w_down, final_norm):
    batch, seq, d = x.shape
    depth = w_in.shape[0]
    fox_heads = b_fgate.shape[1]
    fox_w = fox_heads * FOX_HEAD_DIM
    ret_w = RET_HEADS * RET_HEAD_DIM
    d_ff = w_gate.shape[2]
    assert w_in.shape[2] == 3 * fox_w + fox_heads + 4 * ret_w
    assert fox_w % LANES == 0 and fox_heads <= LANES and d_ff % FF_CHUNK == 0
    assert seq % FOX_TILE == 0 and seq % RET_CHUNK == 0

    h_r = x.reshape(batch * seq, d)
    h_m = jnp.broadcast_to(meta_tokens[None].astype(x.dtype), (batch, N_META, d)).reshape(batch * N_META, d)

    inv_freq = ROPE_BASE ** (-jnp.arange(0, RET_HEAD_DIM, 2, dtype=F32) / RET_HEAD_DIM)
    ang = jnp.arange(seq + N_META, dtype=F32)[:, None] * inv_freq[None, :]
    cos, sin = jnp.cos(ang), jnp.sin(ang)
    cs = jnp.concatenate([cos, cos], axis=1)
    sn = jnp.concatenate([-sin, sin], axis=1)
    cs_r, sn_r = cs[N_META:], sn[N_META:]
    cs_m, sn_m = jnp.tile(cs[:N_META], (batch, 1)), jnp.tile(sn[:N_META], (batch, 1))

    split = 3 * fox_w
    w_all = jnp.concatenate(
        [w_in[..., :split], w_in[..., split + fox_heads:], w_in[..., split:split + fox_heads],
         jnp.zeros((depth, d, LANES - fox_heads), w_in.dtype)], axis=-1).astype(BF16)
    wo, wg, wu, wd = (w.astype(BF16) for w in (w_out, w_gate, w_up, w_down))
    sel = jnp.asarray(_bias_selector(fox_heads), BF16)
    gf = final_norm.reshape(1, d)

    for i in range(depth):
        bl = jnp.pad(b_fgate[i], (0, LANES - fox_heads)).reshape(1, LANES)
        fox_r, ret_r, logf_r, fox_m, ret_m, logf_m = _inproj(
            h_r, h_m, cs_r, sn_r, cs_m, sn_m, attn_norm[i].reshape(1, d), w_all, bl,
            layer=i, seq=seq, fox_w=fox_w, ret_w=ret_w)
        qaug, kaug, qaug_m, kaug_m = _decay(logf_r, logf_m, sel, batch=batch, n_heads=fox_heads)
        fo_r, fo_m = _fox(fox_r, fox_m, qaug, kaug, qaug_m, kaug_m, batch=batch)
        ro_r, ro_m = _retention(ret_r, ret_m, ret_gn[i].reshape(1, ret_w), batch=batch)
        h_r, h_m = _outffn(h_r, fo_r.reshape(batch * seq, fox_w), ro_r.reshape(batch * seq, ret_w),
                           h_m, fo_m.reshape(batch * N_META, fox_w), ro_m.reshape(batch * N_META, ret_w),
                           wo, ffn_norm[i].reshape(1, d), wg, wu, wd, gf,
                           layer=i, final=(i == depth - 1))
    return h_r.reshape(batch, seq, d)
```

```python
import functools
import math

import numpy as np
import jax
import jax.numpy as jnp
from jax import lax
from jax.experimental import pallas as pl
from jax.experimental.pallas import tpu as pltpu

N_META = 16
FOX_HEAD_DIM = 64
RET_HEADS = 4
RET_HEAD_DIM = 128
RET_CHUNK = 128
ROPE_BASE = 10000.0
EPS = 1e-6
RET_LOG_GAMMA = tuple(math.log(1.0 - 2.0 ** (-5 - h)) for h in range(RET_HEADS))
LOG2E = math.log2(math.e)

LANES = 128
BF16_SUBLANES = 16
VMEM_LIMIT_BYTES = 56 * 1024 * 1024

N_SPLIT = 3
BIAS_LANES = 2 * N_SPLIT

INPROJ_ROWS = 1024
OUTFFN_ROWS = 512
FF_CHUNK = 256
FOX_TILE = 256
FOX_SCORE_ROWS = 128
FOX_SLOTS = 6
FOX_PAIRS_PER_STEP = 2

F32 = jnp.float32
BF16 = jnp.bfloat16

assert RET_CHUNK == RET_HEAD_DIM == LANES


def _dot(a, b):
    return jnp.dot(a, b, preferred_element_type=F32)


def _dot_nt(a, b):
    return lax.dot_general(a, b, (((1,), (1,)), ((), ())), preferred_element_type=F32)


def _dot_tn(a, b):
    return lax.dot_general(a, b, (((0,), (0,)), ((), ())), preferred_element_type=F32)


def _rmsnorm(x, g):
    return x * lax.rsqrt(jnp.mean(x * x, axis=-1, keepdims=True) + EPS) * g


def _row_tile(n, cap):
    best = None
    for t in range(BF16_SUBLANES, min(n, cap) + 1, BF16_SUBLANES):
        if n % t == 0:
            best = t
    assert best is not None, n
    return best


def _const_spec(shape):
    nd = len(shape)
    return pl.BlockSpec(shape, lambda *_: (0,) * nd, pipeline_mode=pl.Buffered(1))


def _layer_spec(shape, layer):
    nd = len(shape) - 1
    return pl.BlockSpec((None,) + tuple(shape[1:]), lambda *_: (layer,) + (0,) * nd,
                        pipeline_mode=pl.Buffered(1))


def _params(n_axes):
    return pltpu.CompilerParams(dimension_semantics=("arbitrary",) * n_axes,
                                vmem_limit_bytes=VMEM_LIMIT_BYTES)


def _inproj_body(h_ref, cs_ref, sn_ref, g_ref, w_ref, bl_ref, fox_ref, ret_ref, logf_ref,
                 *, fox_w, ret_w):
    xn = _rmsnorm(h_ref[...], g_ref[...]).astype(BF16)
    q_scale = FOX_HEAD_DIM ** -0.5 * LOG2E
    k_scale = RET_HEAD_DIM ** -0.5
    for c0 in range(0, 3 * fox_w, fox_w):
        y = _dot(xn, w_ref[:, c0:c0 + fox_w])
        if c0 == 0:
            y = y * q_scale
        fox_ref[:, c0:c0 + fox_w] = y.astype(BF16)
    base = 3 * fox_w
    cs = cs_ref[...]
    sn = sn_ref[...]
    for part in range(4):
        y = _dot(xn, w_ref[:, base + part * ret_w:base + (part + 1) * ret_w])
        if part < 2:
            for hh in range(RET_HEADS):
                sl = slice(hh * RET_HEAD_DIM, (hh + 1) * RET_HEAD_DIM)
                yh = y[:, sl]
                yh = yh * cs + pltpu.roll(yh, RET_HEAD_DIM // 2, axis=1) * sn
                if part == 1:
                    yh = yh * k_scale
                ret_ref[:, part * ret_w + sl.start:part * ret_w + sl.stop] = yh.astype(BF16)
        else:
            ret_ref[:, part * ret_w:(part + 1) * ret_w] = y.astype(BF16)
    z = _dot(xn, w_ref[:, base + 4 * ret_w:base + 4 * ret_w + LANES]) + bl_ref[...]
    logf_ref[...] = jnp.minimum(z, 0.0) - jnp.log(1.0 + jnp.exp(-jnp.abs(z)))


def _inproj_kernel(hr_ref, hm_ref, csr_ref, snr_ref, csm_ref, snm_ref, g_ref, w_ref, bl_ref,
                   foxr_ref, retr_ref, logfr_ref, foxm_ref, retm_ref, logfm_ref,
                   *, n_tiles, fox_w, ret_w):
    body = functools.partial(_inproj_body, fox_w=fox_w, ret_w=ret_w)
    t = pl.program_id(0)

    @pl.when(t < n_tiles)
    def _():
        body(hr_ref, csr_ref, snr_ref, g_ref, w_ref, bl_ref, foxr_ref, retr_ref, logfr_ref)

    @pl.when(t == n_tiles)
    def _():
        body(hm_ref, csm_ref, snm_ref, g_ref, w_ref, bl_ref, foxm_ref, retm_ref, logfm_ref)


def _inproj(h_r, h_m, cs_r, sn_r, cs_m, sn_m, g, w_all, bl, *, layer, seq, fox_w, ret_w):
    n_r, d = h_r.shape
    n_m = h_m.shape[0]
    tm = _row_tile(seq, INPROJ_ROWS)
    n_tiles = n_r // tm
    per_seq = seq // tm
    fox_cols, ret_cols = 3 * fox_w, 4 * ret_w

    def real(cols):
        return pl.BlockSpec((tm, cols), lambda t: (jnp.minimum(t, n_tiles - 1), 0))

    def table():
        return pl.BlockSpec((tm, LANES), lambda t: (jnp.minimum(t, n_tiles - 1) % per_seq, 0))

    def meta(cols):
        return pl.BlockSpec((n_m, cols), lambda t: (0, 0))

    kern = functools.partial(_inproj_kernel, n_tiles=n_tiles, fox_w=fox_w, ret_w=ret_w)
    return pl.pallas_call(
        kern,
        grid=(n_tiles + 1,),
        in_specs=[real(d), meta(d), table(), table(), meta(LANES), meta(LANES),
                  _const_spec((1, d)), _layer_spec(w_all.shape, layer), _const_spec((1, LANES))],
        out_specs=[real(fox_cols), real(ret_cols), real(LANES),
                   meta(fox_cols), meta(ret_cols), meta(LANES)],
        out_shape=[
            jax.ShapeDtypeStruct((n_r, fox_cols), BF16),
            jax.ShapeDtypeStruct((n_r, ret_cols), BF16),
            jax.ShapeDtypeStruct((n_r, LANES), F32),
            jax.ShapeDtypeStruct((n_m, fox_cols), BF16),
            jax.ShapeDtypeStruct((n_m, ret_cols), BF16),
            jax.ShapeDtypeStruct((n_m, LANES), F32),
        ],
        compiler_params=_params(1),
        name="inproj",
    )(h_r, h_m, cs_r, sn_r, cs_m, sn_m, g, w_all, bl)


def _bias_lane(pair, a):
    return (FOX_HEAD_DIM if a == 0 else 0) + BIAS_LANES * pair


def _split3(x):
    hi = x.astype(BF16)
    r1 = x - hi.astype(F32)
    mid = r1.astype(BF16)
    lo = (r1 - mid.astype(F32)).astype(BF16)
    return hi, mid, lo


def _bias_selector(n_heads):
    sel = np.zeros((N_SPLIT * LANES, LANES), np.float32)
    for head in range(n_heads):
        lane0 = _bias_lane(head // 2, head % 2)
        for p in range(N_SPLIT):
            sel[p * LANES + head, lane0 + p] = 1.0
            sel[p * LANES + head, lane0 + N_SPLIT + p] = -1.0
    return sel


def _decay_kernel(logf_ref, logfm_ref, sel_ref, qaug_ref, kaug_ref, qaugm_ref, kaugm_ref, c_scr,
                  *, n_heads):
    blk = LANES
    seq = logf_ref.shape[0]
    ri = lax.broadcasted_iota(jnp.int32, (blk, blk), 0)
    ci = lax.broadcasted_iota(jnp.int32, (blk, blk), 1)
    tri = (ci <= ri).astype(BF16)

    def prefix(x):
        hi, mid, lo = _split3(x)
        return _dot(tri, hi) + _dot(tri, mid) + _dot(tri, lo)

    cb = prefix(jnp.concatenate([logfm_ref[...], jnp.zeros((blk - N_META, LANES), F32)], axis=0))
    c_scr[0:blk, :] = cb
    carry = cb[N_META - 1:N_META, :]
    for r0 in range(0, seq, blk):
        cb = prefix(logf_ref[r0:r0 + blk, :]) + carry
        c_scr[blk + r0:2 * blk + r0, :] = cb
        carry = cb[blk - 1:blk, :]

    in_half = lax.broadcasted_iota(jnp.int32, (1, LANES), 1) % FOX_HEAD_DIM
    used = in_half < BIAS_LANES * (n_heads // 2)
    first = used & (in_half % BIAS_LANES < N_SPLIT)
    second = used & (in_half % BIAS_LANES >= N_SPLIT)

    def route(c, q_out, k_out):
        hi, mid, lo = _split3(c * LOG2E)
        routed = _dot(jnp.concatenate([hi, mid, lo], axis=1), sel_ref[...])
        q_out[...] = jnp.where(first, routed, jnp.where(second, 1.0, 0.0)).astype(BF16)
        k_out[...] = jnp.where(second, routed, jnp.where(first, 1.0, 0.0)).astype(BF16)

    route(c_scr[0:N_META, :], qaugm_ref, kaugm_ref)
    route(c_scr[blk:, :], qaug_ref, kaug_ref)


def _decay(logf_r, logf_m, sel, *, batch, n_heads):
    seq = logf_r.shape[0] // batch

    def rows(n):
        return pl.BlockSpec((None, n, LANES), lambda b: (b, 0, 0))

    return pl.pallas_call(
        functools.partial(_decay_kernel, n_heads=n_heads),
        grid=(batch,),
        in_specs=[rows(seq), rows(N_META), _const_spec(sel.shape)],
        out_specs=[rows(seq), rows(seq), rows(N_META), rows(N_META)],
        out_shape=[jax.ShapeDtypeStruct((batch, seq, LANES), BF16)] * 2
        + [jax.ShapeDtypeStruct((batch, N_META, LANES), BF16)] * 2,
        scratch_shapes=[pltpu.VMEM((LANES + seq, LANES), F32)],
        compiler_params=_params(1),
        name="decay",
    )(logf_r.reshape(batch, seq, LANES), logf_m.reshape(batch, N_META, LANES), sel)


def _fox_kernel(q_ref, k_ref, v_ref, qm_ref, km_ref, vm_ref, qaug_ref, kaug_ref, qaugm_ref, kaugm_ref,
                o_ref, om_ref, k_scr, km_scr, vt_scr, vmt_scr, vm_scr, s_scr, sm_scr, p_scr, pm_scr, qt_scr,
                *, tile, score_rows):
    seq = q_ref.shape[0]
    n_qblk = seq // tile
    n_local = q_ref.shape[1] // LANES
    pad_rows = LANES - N_META
    lane = lax.broadcasted_iota(jnp.int32, (1, LANES), 1)
    own = [lane < FOX_HEAD_DIM, lane >= FOX_HEAD_DIM]
    free0 = [FOX_HEAD_DIM, 0]
    n_slots = s_scr.shape[0]
    kv_sets = k_scr.shape[0] // 2

    def lanes_of(j):
        return slice(j * LANES, (j + 1) * LANES)

    def augment(j, a, x, aug):
        lane0 = free0[a] + BIAS_LANES * (pl.program_id(1) * n_local + j)
        mine = (lane >= lane0) & (lane < lane0 + BIAS_LANES)
        return jnp.where(own[a], x, jnp.where(mine, aug, jnp.zeros_like(aug)))

    def prepare(j):
        pad = jnp.zeros((pad_rows, LANES), BF16)
        for a in (0, 1):
            h = 2 * (j % kv_sets) + a
            ones_col = jnp.where(lane == free0[a], 1.0, 0.0).astype(BF16)
            km_scr[h] = jnp.concatenate([augment(j, a, km_ref[:, lanes_of(j)], kaugm_ref[...]), pad], axis=0)
            v_meta = jnp.concatenate([jnp.where(own[a], vm_ref[:, lanes_of(j)], ones_col), pad], axis=0)
            vm_scr[h] = v_meta
            vmt_scr[h] = v_meta.T
            for r0 in range(0, seq, tile):
                rows = slice(r0, r0 + tile)
                k_scr[h, rows, :] = augment(j, a, k_ref[rows, lanes_of(j)], kaug_ref[rows, :])
                vt_scr[h, :, rows] = jnp.where(own[a], v_ref[rows, lanes_of(j)], ones_col).T

    for slot in range(n_slots):
        pm_scr[slot, N_META:LANES, :] = jnp.zeros((pad_rows, tile), BF16)

    def meta_queries(j):
        row_m = lax.broadcasted_iota(jnp.int32, (N_META, LANES), 0)
        col_m = lax.broadcasted_iota(jnp.int32, (N_META, LANES), 1)
        outs = []
        for a in (0, 1):
            h = 2 * (j % kv_sets) + a
            s = _dot_nt(augment(j, a, qm_ref[:, lanes_of(j)], qaugm_ref[...]), km_scr[h])
            s = jnp.where(col_m <= row_m, s, -jnp.inf)
            p = jnp.exp2(s - jnp.max(s, axis=1, keepdims=True)).astype(BF16)
            acc = _dot(p, vm_scr[h])
            outs.append(acc * (1.0 / acc[:, free0[a]:free0[a] + 1]))
        om_ref[:, lanes_of(j)] = jnp.where(own[0], outs[0], outs[1]).astype(om_ref.dtype)

    def score_stage(u, a):
        j, i = divmod(u, n_qblk)
        h = 2 * (j % kv_sets) + a
        slot = (2 * u + a) % n_slots
        d0, kt = i * tile, (i + 1) * tile
        qt_scr[slot] = augment(j, a, q_ref[d0:kt, lanes_of(j)], qaug_ref[d0:kt, :]).T
        qt = qt_scr[slot]
        sm = _dot(km_scr[h, 0:N_META, :], qt)
        sm_scr[slot] = sm
        mx = jnp.max(sm.reshape(N_META // 8, 8, tile), axis=0)
        for r0 in range(0, kt, score_rows):
            rows = min(score_rows, kt - r0)
            s = _dot(k_scr[h, r0:r0 + rows, :], qt)
            if r0 + rows > d0:
                key_i = lax.broadcasted_iota(jnp.int32, (rows, tile), 0) + (r0 - d0)
                qry_i = lax.broadcasted_iota(jnp.int32, (rows, tile), 1)
                s = jnp.where(key_i <= qry_i, s, -jnp.inf)
            s_scr[slot, r0:r0 + rows, :] = s
            mx = jnp.maximum(mx, jnp.max(s.reshape(rows // 8, 8, tile), axis=0))
        return jnp.max(mx, axis=0, keepdims=True)

    def value_stage(u, maxes_u):
        j, i = divmod(u, n_qblk)
        kt = (i + 1) * tile
        outs = []
        for a in (0, 1):
            h = 2 * (j % kv_sets) + a
            slot = (2 * u + a) % n_slots
            pm_scr[slot, 0:N_META, :] = jnp.exp2(sm_scr[slot] - maxes_u[a]).astype(BF16)
            for r0 in range(0, kt, LANES):
                p_scr[slot, r0:r0 + LANES, :] = jnp.exp2(s_scr[slot, r0:r0 + LANES, :] - maxes_u[a]).astype(BF16)
            acc_t = _dot(vt_scr[h, :, 0:kt], p_scr[slot, 0:kt, :]) + _dot(vmt_scr[h], pm_scr[slot])
            l = acc_t[free0[a]:free0[a] + 1, :]
            outs.append((acc_t * (1.0 / l)).T)
        o_ref[i * tile:kt, lanes_of(j)] = jnp.where(own[0], outs[0], outs[1]).astype(o_ref.dtype)

    ahead = n_slots // 2 - 1
    n_units = n_local * n_qblk
    maxes = {}

    def issue_scores(u):
        if u % n_qblk == 0:
            prepare(u // n_qblk)
        maxes[u] = [score_stage(u, a) for a in (0, 1)]

    for u in range(min(ahead, n_units)):
        issue_scores(u)
    for u in range(n_units):
        if u + ahead < n_units:
            issue_scores(u + ahead)
        value_stage(u, maxes.pop(u))
        if u % n_qblk == 0:
            meta_queries(u // n_qblk)


def _fox(fox_r, fox_m, qaug, kaug, qaug_m, kaug_m, *, batch):
    n_r, cols = fox_r.shape
    seq = n_r // batch
    width = cols // 3
    n_pairs = width // LANES
    tile = FOX_TILE
    assert seq % tile == 0 and BIAS_LANES * n_pairs <= FOX_HEAD_DIM
    qkv_r = fox_r.reshape(batch, seq, cols)
    qkv_m = fox_m.reshape(batch, N_META, cols)
    kern = functools.partial(_fox_kernel, tile=tile, score_rows=FOX_SCORE_ROWS)

    per_step = math.gcd(FOX_PAIRS_PER_STEP, n_pairs)
    steps = n_pairs // per_step
    kv = 2 * min(per_step, 2)

    def cols_of(rows, j0):
        return pl.BlockSpec((None, rows, per_step * LANES), lambda b, p: (b, 0, j0 + p))

    def shared(rows):
        return pl.BlockSpec((None, rows, LANES), lambda b, p: (b, 0, 0))

    return pl.pallas_call(
        kern,
        grid=(batch, steps),
        in_specs=[cols_of(seq, 0), cols_of(seq, steps), cols_of(seq, 2 * steps),
                  cols_of(N_META, 0), cols_of(N_META, steps), cols_of(N_META, 2 * steps),
                  shared(seq), shared(seq), shared(N_META), shared(N_META)],
        out_specs=[cols_of(seq, 0), cols_of(N_META, 0)],
        out_shape=[jax.ShapeDtypeStruct((batch, seq, width), BF16),
                   jax.ShapeDtypeStruct((batch, N_META, width), BF16)],
        scratch_shapes=[
            pltpu.VMEM((kv, seq, LANES), BF16),
            pltpu.VMEM((kv, LANES, LANES), BF16),
            pltpu.VMEM((kv, LANES, seq), BF16),
            pltpu.VMEM((kv, LANES, LANES), BF16),
            pltpu.VMEM((kv, LANES, LANES), BF16),
            pltpu.VMEM((FOX_SLOTS, seq, tile), F32),
            pltpu.VMEM((FOX_SLOTS, N_META, tile), F32),
            pltpu.VMEM((FOX_SLOTS, seq, tile), BF16),
            pltpu.VMEM((FOX_SLOTS, LANES, tile), BF16),
            pltpu.VMEM((FOX_SLOTS, LANES, tile), BF16),
        ],
        compiler_params=_params(2),
        name="fox",
    )(qkv_r, qkv_r, qkv_r, qkv_m, qkv_m, qkv_m, qaug, kaug, qaug_m, kaug_m)


def _ret_kernel(q_ref, k_ref, v_ref, g_ref, qm_ref, km_ref, vm_ref, gm_ref, gn_ref,
                o_ref, om_ref, state_scr, dec_scr):
    C = RET_CHUNK
    n_chunks = q_ref.shape[0] // C
    ri = lax.broadcasted_iota(jnp.int32, (C, C), 0)
    ci = lax.broadcasted_iota(jnp.int32, (C, C), 1)
    diff = ri - ci
    row_f = ri.astype(F32)

    for hh in range(RET_HEADS):
        log_g = RET_LOG_GAMMA[hh]
        dec_scr[hh, 0] = jnp.where(diff >= 0, jnp.exp(jnp.maximum(diff, 0).astype(F32) * log_g), 0.0)
        dec_scr[hh, 1] = jnp.exp((row_f + 1.0) * log_g)
        dec_scr[hh, 2] = jnp.exp((C - 1.0 - row_f) * log_g)
        dec_scr[hh, 3] = jnp.exp((N_META - 1.0 - row_f) * log_g)

    def head(hh):
        return slice(hh * RET_HEAD_DIM, (hh + 1) * RET_HEAD_DIM)

    def finish(o, g, out_ref, rows, hh):
        mu = jnp.mean(o, axis=-1, keepdims=True)
        d = o - mu
        var = jnp.mean(d * d, axis=-1, keepdims=True)
        y = d * lax.rsqrt(var + EPS) * gn_ref[:, head(hh)]
        g = g.astype(F32)
        out_ref[rows, head(hh)] = (g * (1.0 / (1.0 + jnp.exp(-g))) * y).astype(out_ref.dtype)

    def intra(q, k, v, hh):
        s = _dot_nt(q, k) * dec_scr[hh, 0]
        return _dot(s.astype(BF16), v)

    def decayed_keys(k, hh, table):
        return (k.astype(F32) * dec_scr[hh, table]).astype(BF16)

    pad = jnp.zeros((C - N_META, RET_HEAD_DIM), BF16)
    for hh in range(RET_HEADS):
        q, k, v = (jnp.concatenate([r[:, head(hh)], pad], axis=0) for r in (qm_ref, km_ref, vm_ref))
        finish(intra(q, k, v, hh)[0:N_META, :], gm_ref[:, head(hh)], om_ref, slice(0, N_META), hh)
        state_scr[hh] = _dot_tn(decayed_keys(k, hh, 3), v)

    def rows_of(c):
        return slice(c * C, (c + 1) * C)

    def scores(c):
        out = []
        for hh in range(RET_HEADS):
            q, k, v = (r[rows_of(c), head(hh)] for r in (q_ref, k_ref, v_ref))
            out.append((_dot_nt(q, k), _dot_tn(decayed_keys(k, hh, 2), v)))
        return out

    def carry(c, kv):
        out = []
        for hh in range(RET_HEADS):
            state = state_scr[hh]
            out.append(_dot(q_ref[rows_of(c), head(hh)], state.astype(BF16)) * dec_scr[hh, 1])
            chunk_dec = jnp.exp(jnp.full((1, 1), C * RET_LOG_GAMMA[hh], F32))
            state_scr[hh] = chunk_dec * state + kv[hh]
        return out

    def outputs(c, s, inter):
        for hh in range(RET_HEADS):
            o = _dot((s[hh] * dec_scr[hh, 0]).astype(BF16), v_ref[rows_of(c), head(hh)]) + inter[hh]
            finish(o, g_ref[rows_of(c), head(hh)], o_ref, rows_of(c), hh)

    cur = scores(0)
    inter = carry(0, [kv for _, kv in cur])
    for c in range(n_chunks):
        nxt = scores(c + 1) if c + 1 < n_chunks else None
        outputs(c, [s for s, _ in cur], inter)
        if nxt is not None:
            inter = carry(c + 1, [kv for _, kv in nxt])
        cur = nxt


def _retention(ret_r, ret_m, gn, *, batch):
    n_r, cols = ret_r.shape
    seq = n_r // batch
    width = cols // 4
    x_r = ret_r.reshape(batch, seq, cols)
    x_m = ret_m.reshape(batch, N_META, cols)

    def part(rows, j):
        return pl.BlockSpec((None, rows, width), lambda b: (b, 0, j))

    return pl.pallas_call(
        _ret_kernel,
        grid=(batch,),
        in_specs=[part(seq, j) for j in range(4)] + [part(N_META, j) for j in range(4)]
        + [_const_spec(gn.shape)],
        out_specs=[part(seq, 0), part(N_META, 0)],
        out_shape=[jax.ShapeDtypeStruct((batch, seq, width), BF16),
                   jax.ShapeDtypeStruct((batch, N_META, width), BF16)],
        scratch_shapes=[pltpu.VMEM((RET_HEADS, RET_HEAD_DIM, RET_HEAD_DIM), F32),
                        pltpu.VMEM((RET_HEADS, 4, RET_CHUNK, RET_HEAD_DIM), F32)],
        compiler_params=_params(1),
        name="retention",
    )(x_r, x_r, x_r, x_r, x_m, x_m, x_m, x_m, gn)


def _outffn_body(h_ref, fox_ref, ret_ref, wo_ref, g2_ref, wg_ref, wu_ref, wd_ref, gf_ref, o_ref,
                 hn_scr, act_scr, *, final):
    rows = h_ref.shape[0]
    fox_w = fox_ref.shape[1]
    h1 = h_ref[...] + _dot(fox_ref[...], wo_ref[0:fox_w, :]) + _dot(ret_ref[...], wo_ref[fox_w:, :])
    hn_scr[0:rows, :] = _rmsnorm(h1, g2_ref[...]).astype(BF16)
    o_ref[...] = h1
    for c0 in range(0, wg_ref.shape[1], FF_CHUNK):
        hn = hn_scr[0:rows, :]
        gate = _dot(hn, wg_ref[:, c0:c0 + FF_CHUNK])
        up = _dot(hn, wu_ref[:, c0:c0 + FF_CHUNK])
        act_scr[0:rows, c0:c0 + FF_CHUNK] = (gate * (1.0 / (1.0 + jnp.exp(-gate))) * up).astype(BF16)
    out = o_ref[...] + _dot(act_scr[0:rows, :], wd_ref[...])
    if final:
        out = _rmsnorm(out, gf_ref[...])
    o_ref[...] = out


def _outffn_kernel(*refs, n_tiles, final):
    if final:
        (hr, foxr, retr, wo, g2, wg, wu, wd, gf, outr, hn_scr, act_scr) = refs
    else:
        (hr, foxr, retr, hm, foxm, retm, wo, g2, wg, wu, wd, gf, outr, outm, hn_scr, act_scr) = refs
    body = functools.partial(_outffn_body, final=final)
    t = pl.program_id(0)

    @pl.when(t < n_tiles)
    def _():
        body(hr, foxr, retr, wo, g2, wg, wu, wd, gf, outr, hn_scr, act_scr)

    if not final:
        @pl.when(t == n_tiles)
        def _():
            body(hm, foxm, retm, wo, g2, wg, wu, wd, gf, outm, hn_scr, act_scr)


def _outffn(h_r, fox_r, ret_r, h_m, fox_m, ret_m, wo, g2, wg, wu, wd, gf, *, layer, final):
    n_r, d = h_r.shape
    n_m = h_m.shape[0]
    d_ff = wg.shape[2]
    tm = _row_tile(n_r, OUTFFN_ROWS)
    n_tiles = n_r // tm
    assert n_m <= tm

    def real(cols):
        return pl.BlockSpec((tm, cols), lambda t: (jnp.minimum(t, n_tiles - 1), 0))

    def meta(cols):
        return pl.BlockSpec((n_m, cols), lambda t: (0, 0))

    weights = [_layer_spec(wo.shape, layer), _const_spec(g2.shape), _layer_spec(wg.shape, layer),
               _layer_spec(wu.shape, layer), _layer_spec(wd.shape, layer), _const_spec(gf.shape)]
    real_in = [real(d), real(fox_r.shape[1]), real(ret_r.shape[1])]
    kern = functools.partial(_outffn_kernel, n_tiles=n_tiles, final=final)
    scratch = [pltpu.VMEM((tm, d), BF16),
               pltpu.VMEM((tm, d_ff), BF16)]
    if final:
        return pl.pallas_call(
            kern, grid=(n_tiles,), in_specs=real_in + weights, out_specs=real(d),
            out_shape=jax.ShapeDtypeStruct((n_r, d), F32), scratch_shapes=scratch,
            compiler_params=_params(1), name="outffn_last",
        )(h_r, fox_r, ret_r, wo, g2, wg, wu, wd, gf), None
    meta_in = [meta(d), meta(fox_m.shape[1]), meta(ret_m.shape[1])]
    return pl.pallas_call(
        kern, grid=(n_tiles + 1,), in_specs=real_in + meta_in + weights,
        out_specs=[real(d), meta(d)],
        out_shape=[jax.ShapeDtypeStruct((n_r, d), F32), jax.ShapeDtypeStruct((n_m, d), F32)],
        scratch_shapes=scratch, compiler_params=_params(1), name="outffn",
    )(h_r, fox_r, ret_r, h_m, fox_m, ret_m, wo, g2, wg, wu, wd, gf)


def kernel(x, meta_tokens, attn_norm, w_in, b_fgate, ret_gn, w_out, ffn_norm,
           w_gate, w_up, w_down, final_norm):
    batch, seq, d = x.shape
    depth = w_in.shape[0]
    fox_heads = b_fgate.shape[1]
    fox_w = fox_heads * FOX_HEAD_DIM
    ret_w = RET_HEADS * RET_HEAD_DIM
    d_ff = w_gate.shape[2]
    assert w_in.shape[2] == 3 * fox_w + fox_heads + 4 * ret_w
    assert fox_w % LANES == 0 and fox_heads <= LANES and d_ff % FF_CHUNK == 0
    assert seq % FOX_TILE == 0 and seq % RET_CHUNK == 0

    h_r = x.reshape(batch * seq, d)
    h_m = jnp.broadcast_to(meta_tokens[None].astype(x.dtype), (batch, N_META, d)).reshape(batch * N_META, d)

    inv_freq = ROPE_BASE ** (-jnp.arange(0, RET_HEAD_DIM, 2, dtype=F32) / RET_HEAD_DIM)
    ang = jnp.arange(seq + N_META, dtype=F32)[:, None] * inv_freq[None, :]
    cos, sin = jnp.cos(ang), jnp.sin(ang)
    cs = jnp.concatenate([cos, cos], axis=1)
    sn = jnp.concatenate([-sin, sin], axis=1)
    cs_r, sn_r = cs[N_META:], sn[N_META:]
    cs_m, sn_m = jnp.tile(cs[:N_META], (batch, 1)), jnp.tile(sn[:N_META], (batch, 1))

    split = 3 * fox_w
    w_all = jnp.concatenate(
        [w_in[..., :split], w_in[..., split + fox_heads:], w_in[..., split:split + fox_heads],
         jnp.zeros((depth, d, LANES - fox_heads), w_in.dtype)], axis=-1).astype(BF16)
    wo, wg, wu, wd = (w.astype(BF16) for w in (w_out, w_gate, w_up, w_down))
    sel = jnp.asarray(_bias_selector(fox_heads), BF16)
    gf = final_norm.reshape(1, d)

    for i in range(depth):
        bl = jnp.pad(b_fgate[i], (0, LANES - fox_heads)).reshape(1, LANES)
        fox_r, ret_r, logf_r, fox_m, ret_m, logf_m = _inproj(
            h_r, h_m, cs_r, sn_r, cs_m, sn_m, attn_norm[i].reshape(1, d), w_all, bl,
            layer=i, seq=seq, fox_w=fox_w, ret_w=ret_w)
        qaug, kaug, qaug_m, kaug_m = _decay(logf_r, logf_m, sel, batch=batch, n_heads=fox_heads)
        fo_r, fo_m = _fox(fox_r, fox_m, qaug, kaug, qaug_m, kaug_m, batch=batch)
        ro_r, ro_m = _retention(ret_r, ret_m, ret_gn[i].reshape(1, ret_w), batch=batch)
        h_r, h_m = _outffn(h_r, fo_r.reshape(batch * seq, fox_w), ro_r.reshape(batch * seq, ret_w),
                           h_m, fo_m.reshape(batch * N_META, fox_w), ro_m.reshape(batch * N_META, ret_w),
                           wo, ffn_norm[i].reshape(1, d), wg, wu, wd, gf,
                           layer=i, final=(i == depth - 1))
    return h_r.reshape(batch, seq, d)
```

```python
import functools
import math

import numpy as np
import jax
import jax.numpy as jnp
from jax import lax
from jax.experimental import pallas as pl
from jax.experimental.pallas import tpu as pltpu

N_META = 16
FOX_HEAD_DIM = 64
RET_HEADS = 4
RET_HEAD_DIM = 128
RET_CHUNK = 128
ROPE_BASE = 10000.0
EPS = 1e-6
RET_LOG_GAMMA = tuple(math.log(1.0 - 2.0 ** (-5 - h)) for h in range(RET_HEADS))
LOG2E = math.log2(math.e)

LANES = 128
BF16_SUBLANES = 16
VMEM_LIMIT_BYTES = 56 * 1024 * 1024

N_SPLIT = 3
BIAS_LANES = 2 * N_SPLIT

INPROJ_ROWS = 1024
OUTFFN_ROWS = 512
FF_CHUNK = 256
FOX_TILE = 256
FOX_SCORE_ROWS = 128
FOX_SLOTS = 6

F32 = jnp.float32
BF16 = jnp.bfloat16

assert RET_CHUNK == RET_HEAD_DIM == LANES


def _dot(a, b):
    return jnp.dot(a, b, preferred_element_type=F32)


def _dot_nt(a, b):
    return lax.dot_general(a, b, (((1,), (1,)), ((), ())), preferred_element_type=F32)


def _dot_tn(a, b):
    return lax.dot_general(a, b, (((0,), (0,)), ((), ())), preferred_element_type=F32)


def _rmsnorm(x, g):
    return x * lax.rsqrt(jnp.mean(x * x, axis=-1, keepdims=True) + EPS) * g


def _row_tile(n, cap):
    best = None
    for t in range(BF16_SUBLANES, min(n, cap) + 1, BF16_SUBLANES):
        if n % t == 0:
            best = t
    assert best is not None, n
    return best


def _const_spec(shape):
    nd = len(shape)
    return pl.BlockSpec(shape, lambda *_: (0,) * nd, pipeline_mode=pl.Buffered(1))


def _layer_spec(shape, layer):
    nd = len(shape) - 1
    return pl.BlockSpec((None,) + tuple(shape[1:]), lambda *_: (layer,) + (0,) * nd,
                        pipeline_mode=pl.Buffered(1))


def _params(n_axes):
    return pltpu.CompilerParams(dimension_semantics=("arbitrary",) * n_axes,
                                vmem_limit_bytes=VMEM_LIMIT_BYTES)


def _inproj_body(h_ref, cs_ref, sn_ref, g_ref, w_ref, bl_ref, fox_ref, ret_ref, logf_ref, kt_ref=None,
                 *, fox_w, ret_w):
    ret_col = {0: 0, 1: ret_w, 2: 2 * ret_w, 3: 3 * ret_w} if kt_ref is None else {0: 0, 2: ret_w, 3: 2 * ret_w}
    xn = _rmsnorm(h_ref[...], g_ref[...]).astype(BF16)
    q_scale = FOX_HEAD_DIM ** -0.5 * LOG2E
    k_scale = RET_HEAD_DIM ** -0.5
    for c0 in range(0, 3 * fox_w, fox_w):
        y = _dot(xn, w_ref[:, c0:c0 + fox_w])
        if c0 == 0:
            y = y * q_scale
        fox_ref[:, c0:c0 + fox_w] = y.astype(BF16)
    base = 3 * fox_w
    cs = cs_ref[...]
    sn = sn_ref[...]
    for part in range(4):
        y = _dot(xn, w_ref[:, base + part * ret_w:base + (part + 1) * ret_w])
        if part < 2:
            for hh in range(RET_HEADS):
                sl = slice(hh * RET_HEAD_DIM, (hh + 1) * RET_HEAD_DIM)
                yh = y[:, sl]
                yh = yh * cs + pltpu.roll(yh, RET_HEAD_DIM // 2, axis=1) * sn
                if part == 1:
                    yh = yh * k_scale
                if part == 1 and kt_ref is not None:
                    kt_ref[sl, :] = yh.astype(BF16).T
                else:
                    ret_ref[:, ret_col[part] + sl.start:ret_col[part] + sl.stop] = yh.astype(BF16)
        else:
            ret_ref[:, ret_col[part]:ret_col[part] + ret_w] = y.astype(BF16)
    z = _dot(xn, w_ref[:, base + 4 * ret_w:base + 4 * ret_w + LANES]) + bl_ref[...]
    logf_ref[...] = jnp.minimum(z, 0.0) - jnp.log(1.0 + jnp.exp(-jnp.abs(z)))


def _inproj_kernel(hr_ref, hm_ref, csr_ref, snr_ref, csm_ref, snm_ref, g_ref, w_ref, bl_ref,
                   foxr_ref, retr_ref, logfr_ref, ktr_ref, foxm_ref, retm_ref, logfm_ref,
                   *, n_tiles, fox_w, ret_w):
    body = functools.partial(_inproj_body, fox_w=fox_w, ret_w=ret_w)
    t = pl.program_id(0)

    @pl.when(t < n_tiles)
    def _():
        body(hr_ref, csr_ref, snr_ref, g_ref, w_ref, bl_ref, foxr_ref, retr_ref, logfr_ref, ktr_ref)

    @pl.when(t == n_tiles)
    def _():
        body(hm_ref, csm_ref, snm_ref, g_ref, w_ref, bl_ref, foxm_ref, retm_ref, logfm_ref)


def _inproj(h_r, h_m, cs_r, sn_r, cs_m, sn_m, g, w_all, bl, *, layer, seq, fox_w, ret_w):
    n_r, d = h_r.shape
    n_m = h_m.shape[0]
    tm = _row_tile(seq, INPROJ_ROWS)
    n_tiles = n_r // tm
    per_seq = seq // tm
    fox_cols, ret_cols = 3 * fox_w, 4 * ret_w

    def real(cols):
        return pl.BlockSpec((tm, cols), lambda t: (jnp.minimum(t, n_tiles - 1), 0))

    def table():
        return pl.BlockSpec((tm, LANES), lambda t: (jnp.minimum(t, n_tiles - 1) % per_seq, 0))

    def meta(cols):
        return pl.BlockSpec((n_m, cols), lambda t: (0, 0))

    def transposed_keys():
        return pl.BlockSpec((None, ret_w, tm), lambda t: (jnp.minimum(t, n_tiles - 1) // per_seq, 0,
                                                           jnp.minimum(t, n_tiles - 1) % per_seq))

    kern = functools.partial(_inproj_kernel, n_tiles=n_tiles, fox_w=fox_w, ret_w=ret_w)
    return pl.pallas_call(
        kern,
        grid=(n_tiles + 1,),
        in_specs=[real(d), meta(d), table(), table(), meta(LANES), meta(LANES),
                  _const_spec((1, d)), _layer_spec(w_all.shape, layer), _const_spec((1, LANES))],
        out_specs=[real(fox_cols), real(ret_cols - ret_w), real(LANES), transposed_keys(),
                   meta(fox_cols), meta(ret_cols), meta(LANES)],
        out_shape=[
            jax.ShapeDtypeStruct((n_r, fox_cols), BF16),
            jax.ShapeDtypeStruct((n_r, ret_cols - ret_w), BF16),
            jax.ShapeDtypeStruct((n_r, LANES), F32),
            jax.ShapeDtypeStruct((n_r // seq, ret_w, seq), BF16),
            jax.ShapeDtypeStruct((n_m, fox_cols), BF16),
            jax.ShapeDtypeStruct((n_m, ret_cols), BF16),
            jax.ShapeDtypeStruct((n_m, LANES), F32),
        ],
        compiler_params=_params(1),
        name="inproj",
    )(h_r, h_m, cs_r, sn_r, cs_m, sn_m, g, w_all, bl)


def _bias_lane(pair, a):
    return (FOX_HEAD_DIM if a == 0 else 0) + BIAS_LANES * pair


def _split3(x):
    hi = x.astype(BF16)
    r1 = x - hi.astype(F32)
    mid = r1.astype(BF16)
    lo = (r1 - mid.astype(F32)).astype(BF16)
    return hi, mid, lo


def _bias_selector(n_heads):
    sel = np.zeros((N_SPLIT * LANES, LANES), np.float32)
    for head in range(n_heads):
        lane0 = _bias_lane(head // 2, head % 2)
        for p in range(N_SPLIT):
            sel[p * LANES + head, lane0 + p] = 1.0
            sel[p * LANES + head, lane0 + N_SPLIT + p] = -1.0
    return sel


def _decay_kernel(logf_ref, logfm_ref, sel_ref, qaug_ref, kaug_ref, qaugm_ref, kaugm_ref, c_scr,
                  *, n_heads):
    blk = LANES
    seq = logf_ref.shape[0]
    ri = lax.broadcasted_iota(jnp.int32, (blk, blk), 0)
    ci = lax.broadcasted_iota(jnp.int32, (blk, blk), 1)
    tri = (ci <= ri).astype(BF16)

    def prefix(x):
        hi, mid, lo = _split3(x)
        return _dot(tri, hi) + _dot(tri, mid) + _dot(tri, lo)

    cb = prefix(jnp.concatenate([logfm_ref[...], jnp.zeros((blk - N_META, LANES), F32)], axis=0))
    c_scr[0:blk, :] = cb
    carry = cb[N_META - 1:N_META, :]
    for r0 in range(0, seq, blk):
        cb = prefix(logf_ref[r0:r0 + blk, :]) + carry
        c_scr[blk + r0:2 * blk + r0, :] = cb
        carry = cb[blk - 1:blk, :]

    in_half = lax.broadcasted_iota(jnp.int32, (1, LANES), 1) % FOX_HEAD_DIM
    used = in_half < BIAS_LANES * (n_heads // 2)
    first = used & (in_half % BIAS_LANES < N_SPLIT)
    second = used & (in_half % BIAS_LANES >= N_SPLIT)

    def route(c, q_out, k_out):
        hi, mid, lo = _split3(c * LOG2E)
        routed = _dot(jnp.concatenate([hi, mid, lo], axis=1), sel_ref[...])
        q_out[...] = jnp.where(first, routed, jnp.where(second, 1.0, 0.0)).astype(BF16)
        k_out[...] = jnp.where(second, routed, jnp.where(first, 1.0, 0.0)).astype(BF16)

    route(c_scr[0:N_META, :], qaugm_ref, kaugm_ref)
    route(c_scr[blk:, :], qaug_ref, kaug_ref)


def _decay(logf_r, logf_m, sel, *, batch, n_heads):
    seq = logf_r.shape[0] // batch

    def rows(n):
        return pl.BlockSpec((None, n, LANES), lambda b: (b, 0, 0))

    return pl.pallas_call(
        functools.partial(_decay_kernel, n_heads=n_heads),
        grid=(batch,),
        in_specs=[rows(seq), rows(N_META), _const_spec(sel.shape)],
        out_specs=[rows(seq), rows(seq), rows(N_META), rows(N_META)],
        out_shape=[jax.ShapeDtypeStruct((batch, seq, LANES), BF16)] * 2
        + [jax.ShapeDtypeStruct((batch, N_META, LANES), BF16)] * 2,
        scratch_shapes=[pltpu.VMEM((LANES + seq, LANES), F32)],
        compiler_params=_params(1),
        name="decay",
    )(logf_r.reshape(batch, seq, LANES), logf_m.reshape(batch, N_META, LANES), sel)


def _fox_kernel(q_ref, k_ref, v_ref, qm_ref, km_ref, vm_ref, qaug_ref, kaug_ref, qaugm_ref, kaugm_ref,
                o_ref, om_ref, k_scr, km_scr, vt_scr, vmt_scr, vm_scr, s_scr, sm_scr, p_scr, pm_scr, qt_scr,
                *, tile, score_rows):
    seq = q_ref.shape[0]
    n_qblk = seq // tile
    pad_rows = LANES - N_META
    pair = pl.program_id(1)
    lane = lax.broadcasted_iota(jnp.int32, (1, LANES), 1)
    own = [lane < FOX_HEAD_DIM, lane >= FOX_HEAD_DIM]
    free0 = [FOX_HEAD_DIM, 0]
    n_slots = s_scr.shape[0]

    def augment(a, x, aug):
        lane0 = free0[a] + BIAS_LANES * pair
        mine = (lane >= lane0) & (lane < lane0 + BIAS_LANES)
        return jnp.where(own[a], x, jnp.where(mine, aug, jnp.zeros_like(aug)))

    pad = jnp.zeros((pad_rows, LANES), BF16)
    for a in (0, 1):
        ones_col = jnp.where(lane == free0[a], 1.0, 0.0).astype(BF16)
        km_scr[a] = jnp.concatenate([augment(a, km_ref[...], kaugm_ref[...]), pad], axis=0)
        v_meta = jnp.concatenate([jnp.where(own[a], vm_ref[...], ones_col), pad], axis=0)
        vm_scr[a] = v_meta
        vmt_scr[a] = v_meta.T
        for r0 in range(0, seq, tile):
            rows = slice(r0, r0 + tile)
            k_scr[a, rows, :] = augment(a, k_ref[rows, :], kaug_ref[rows, :])
            vt_scr[a, :, rows] = jnp.where(own[a], v_ref[rows, :], ones_col).T
    for slot in range(n_slots):
        pm_scr[slot, N_META:LANES, :] = jnp.zeros((pad_rows, tile), BF16)

    def meta_queries():
        row_m = lax.broadcasted_iota(jnp.int32, (N_META, LANES), 0)
        col_m = lax.broadcasted_iota(jnp.int32, (N_META, LANES), 1)
        outs = []
        for a in (0, 1):
            s = _dot_nt(augment(a, qm_ref[...], qaugm_ref[...]), km_scr[a])
            s = jnp.where(col_m <= row_m, s, -jnp.inf)
            p = jnp.exp2(s - jnp.max(s, axis=1, keepdims=True)).astype(BF16)
            acc = _dot(p, vm_scr[a])
            outs.append(acc * (1.0 / acc[:, free0[a]:free0[a] + 1]))
        om_ref[...] = jnp.where(own[0], outs[0], outs[1]).astype(om_ref.dtype)

    def score_stage(i, a):
        slot = (2 * i + a) % n_slots
        d0, kt = i * tile, (i + 1) * tile
        qt_scr[slot] = augment(a, q_ref[d0:kt, :], qaug_ref[d0:kt, :]).T
        qt = qt_scr[slot]
        sm = _dot(km_scr[a, 0:N_META, :], qt)
        sm_scr[slot] = sm
        mx = jnp.max(sm.reshape(N_META // 8, 8, tile), axis=0)
        for r0 in range(0, kt, score_rows):
            rows = min(score_rows, kt - r0)
            s = _dot(k_scr[a, r0:r0 + rows, :], qt)
            if r0 + rows > d0:
                key_i = lax.broadcasted_iota(jnp.int32, (rows, tile), 0) + (r0 - d0)
                qry_i = lax.broadcasted_iota(jnp.int32, (rows, tile), 1)
                s = jnp.where(key_i <= qry_i, s, -jnp.inf)
            s_scr[slot, r0:r0 + rows, :] = s
            mx = jnp.maximum(mx, jnp.max(s.reshape(rows // 8, 8, tile), axis=0))
        return jnp.max(mx, axis=0, keepdims=True)

    def value_stage(i, maxes_i):
        kt = (i + 1) * tile
        slots = [(2 * i + a) % n_slots for a in (0, 1)]
        for a in (0, 1):
            pm_scr[slots[a], 0:N_META, :] = jnp.exp2(sm_scr[slots[a]] - maxes_i[a]).astype(BF16)
        for r0 in range(0, kt, LANES):
            for a in (0, 1):
                p_scr[slots[a], r0:r0 + LANES, :] = jnp.exp2(
                    s_scr[slots[a], r0:r0 + LANES, :] - maxes_i[a]).astype(BF16)
        outs = []
        for a in (0, 1):
            acc_t = (_dot(vt_scr[a, :, 0:kt], p_scr[slots[a], 0:kt, :])
                     + _dot(vmt_scr[a], pm_scr[slots[a]]))
            l = acc_t[free0[a]:free0[a] + 1, :]
            outs.append((acc_t * (1.0 / l)).T)
        return outs

    ahead = n_slots // 2 - 1
    maxes = {j: [score_stage(j, a) for a in (0, 1)] for j in range(min(ahead, n_qblk))}
    for i in range(n_qblk):
        if i + ahead < n_qblk:
            maxes[i + ahead] = [score_stage(i + ahead, a) for a in (0, 1)]
        outs = value_stage(i, maxes[i])
        o_ref[i * tile:(i + 1) * tile, :] = jnp.where(own[0], outs[0], outs[1]).astype(o_ref.dtype)
        if i == 0:
            meta_queries()


def _fox(fox_r, fox_m, qaug, kaug, qaug_m, kaug_m, *, batch):
    n_r, cols = fox_r.shape
    seq = n_r // batch
    width = cols // 3
    n_pairs = width // LANES
    tile = FOX_TILE
    assert seq % tile == 0 and BIAS_LANES * n_pairs <= FOX_HEAD_DIM
    qkv_r = fox_r.reshape(batch, seq, cols)
    qkv_m = fox_m.reshape(batch, N_META, cols)
    kern = functools.partial(_fox_kernel, tile=tile, score_rows=FOX_SCORE_ROWS)

    def cols_of(rows, j0):
        return pl.BlockSpec((None, rows, LANES), lambda b, p: (b, 0, j0 + p))

    def shared(rows):
        return pl.BlockSpec((None, rows, LANES), lambda b, p: (b, 0, 0))

    return pl.pallas_call(
        kern,
        grid=(batch, n_pairs),
        in_specs=[cols_of(seq, 0), cols_of(seq, n_pairs), cols_of(seq, 2 * n_pairs),
                  cols_of(N_META, 0), cols_of(N_META, n_pairs), cols_of(N_META, 2 * n_pairs),
                  shared(seq), shared(seq), shared(N_META), shared(N_META)],
        out_specs=[cols_of(seq, 0), cols_of(N_META, 0)],
        out_shape=[jax.ShapeDtypeStruct((batch, seq, width), BF16),
                   jax.ShapeDtypeStruct((batch, N_META, width), BF16)],
        scratch_shapes=[
            pltpu.VMEM((2, seq, LANES), BF16),
            pltpu.VMEM((2, LANES, LANES), BF16),
            pltpu.VMEM((2, LANES, seq), BF16),
            pltpu.VMEM((2, LANES, LANES), BF16),
            pltpu.VMEM((2, LANES, LANES), BF16),
            pltpu.VMEM((FOX_SLOTS, seq, tile), F32),
            pltpu.VMEM((FOX_SLOTS, N_META, tile), F32),
            pltpu.VMEM((FOX_SLOTS, seq, tile), BF16),
            pltpu.VMEM((FOX_SLOTS, LANES, tile), BF16),
            pltpu.VMEM((FOX_SLOTS, LANES, tile), BF16),
        ],
        compiler_params=_params(2),
        name="fox",
    )(qkv_r, qkv_r, qkv_r, qkv_m, qkv_m, qkv_m, qaug, kaug, qaug_m, kaug_m)


def _ret_kernel(q_ref, kt_ref, v_ref, g_ref, qm_ref, km_ref, vm_ref, gm_ref, gn_ref,
                o_ref, om_ref, state_scr, dec_scr):
    C = RET_CHUNK
    n_chunks = q_ref.shape[0] // C
    ri = lax.broadcasted_iota(jnp.int32, (C, C), 0)
    ci = lax.broadcasted_iota(jnp.int32, (C, C), 1)
    diff = ri - ci
    row_f = ri.astype(F32)

    for hh in range(RET_HEADS):
        log_g = RET_LOG_GAMMA[hh]
        dec_scr[hh, 0] = jnp.where(diff >= 0, jnp.exp(jnp.maximum(diff, 0).astype(F32) * log_g), 0.0)
        dec_scr[hh, 1] = jnp.exp((row_f + 1.0) * log_g)
        dec_scr[hh, 2] = jnp.exp((C - 1.0 - ci.astype(F32)) * log_g)
        dec_scr[hh, 3] = jnp.exp((N_META - 1.0 - row_f) * log_g)

    def head(hh):
        return slice(hh * RET_HEAD_DIM, (hh + 1) * RET_HEAD_DIM)

    def finish(o, g, out_ref, rows, hh):
        mu = jnp.mean(o, axis=-1, keepdims=True)
        d = o - mu
        var = jnp.mean(d * d, axis=-1, keepdims=True)
        y = d * lax.rsqrt(var + EPS) * gn_ref[:, head(hh)]
        g = g.astype(F32)
        out_ref[rows, head(hh)] = (g * (1.0 / (1.0 + jnp.exp(-g))) * y).astype(out_ref.dtype)

    def intra(q, k, v, hh):
        s = _dot_nt(q, k) * dec_scr[hh, 0]
        return _dot(s.astype(BF16), v)

    def decayed_keys(k, hh, table):
        return (k.astype(F32) * dec_scr[hh, table]).astype(BF16)

    pad = jnp.zeros((C - N_META, RET_HEAD_DIM), BF16)
    for hh in range(RET_HEADS):
        q, k, v = (jnp.concatenate([r[:, head(hh)], pad], axis=0) for r in (qm_ref, km_ref, vm_ref))
        finish(intra(q, k, v, hh)[0:N_META, :], gm_ref[:, head(hh)], om_ref, slice(0, N_META), hh)
        state_scr[hh] = _dot_tn(decayed_keys(k, hh, 3), v)

    def rows_of(c):
        return slice(c * C, (c + 1) * C)

    def scores(c):
        out = []
        for hh in range(RET_HEADS):
            q, v = q_ref[rows_of(c), head(hh)], v_ref[rows_of(c), head(hh)]
            k_t = kt_ref[head(hh), rows_of(c)]
            out.append((_dot(q, k_t), _dot(decayed_keys(k_t, hh, 2), v)))
        return out

    def carry(c, kv):
        out = []
        for hh in range(RET_HEADS):
            state = state_scr[hh]
            out.append(_dot(q_ref[rows_of(c), head(hh)], state.astype(BF16)) * dec_scr[hh, 1])
            chunk_dec = jnp.exp(jnp.full((1, 1), C * RET_LOG_GAMMA[hh], F32))
            state_scr[hh] = chunk_dec * state + kv[hh]
        return out

    def outputs(c, s, inter):
        for hh in range(RET_HEADS):
            o = _dot((s[hh] * dec_scr[hh, 0]).astype(BF16), v_ref[rows_of(c), head(hh)]) + inter[hh]
            finish(o, g_ref[rows_of(c), head(hh)], o_ref, rows_of(c), hh)

    cur = scores(0)
    inter = carry(0, [kv for _, kv in cur])
    for c in range(n_chunks):
        nxt = scores(c + 1) if c + 1 < n_chunks else None
        outputs(c, [s for s, _ in cur], inter)
        if nxt is not None:
            inter = carry(c + 1, [kv for _, kv in nxt])
        cur = nxt


def _retention(ret_r, kt_r, ret_m, gn, *, batch):
    n_r, cols = ret_r.shape
    seq = n_r // batch
    width = cols // 3
    x_r = ret_r.reshape(batch, seq, cols)
    x_m = ret_m.reshape(batch, N_META, 4 * width)

    def part(rows, j):
        return pl.BlockSpec((None, rows, width), lambda b: (b, 0, j))

    return pl.pallas_call(
        _ret_kernel,
        grid=(batch,),
        in_specs=[part(seq, 0), pl.BlockSpec((None, width, seq), lambda b: (b, 0, 0)), part(seq, 1), part(seq, 2)]
        + [part(N_META, j) for j in range(4)] + [_const_spec(gn.shape)],
        out_specs=[part(seq, 0), part(N_META, 0)],
        out_shape=[jax.ShapeDtypeStruct((batch, seq, width), BF16),
                   jax.ShapeDtypeStruct((batch, N_META, width), BF16)],
        scratch_shapes=[pltpu.VMEM((RET_HEADS, RET_HEAD_DIM, RET_HEAD_DIM), F32),
                        pltpu.VMEM((RET_HEADS, 4, RET_CHUNK, RET_HEAD_DIM), F32)],
        compiler_params=_params(1),
        name="retention",
    )(x_r, kt_r, x_r, x_r, x_m, x_m, x_m, x_m, gn)


def _outffn_body(h_ref, fox_ref, ret_ref, wo_ref, g2_ref, wg_ref, wu_ref, wd_ref, gf_ref, o_ref,
                 hn_scr, act_scr, *, final):
    rows = h_ref.shape[0]
    fox_w = fox_ref.shape[1]
    h1 = h_ref[...] + _dot(fox_ref[...], wo_ref[0:fox_w, :]) + _dot(ret_ref[...], wo_ref[fox_w:, :])
    hn_scr[0:rows, :] = _rmsnorm(h1, g2_ref[...]).astype(BF16)
    o_ref[...] = h1
    for c0 in range(0, wg_ref.shape[1], FF_CHUNK):
        hn = hn_scr[0:rows, :]
        gate = _dot(hn, wg_ref[:, c0:c0 + FF_CHUNK])
        up = _dot(hn, wu_ref[:, c0:c0 + FF_CHUNK])
        act_scr[0:rows, c0:c0 + FF_CHUNK] = (gate * (1.0 / (1.0 + jnp.exp(-gate))) * up).astype(BF16)
    out = o_ref[...] + _dot(act_scr[0:rows, :], wd_ref[...])
    if final:
        out = _rmsnorm(out, gf_ref[...])
    o_ref[...] = out


def _outffn_kernel(*refs, n_tiles, final):
    if final:
        (hr, foxr, retr, wo, g2, wg, wu, wd, gf, outr, hn_scr, act_scr) = refs
    else:
        (hr, foxr, retr, hm, foxm, retm, wo, g2, wg, wu, wd, gf, outr, outm, hn_scr, act_scr) = refs
    body = functools.partial(_outffn_body, final=final)
    t = pl.program_id(0)

    @pl.when(t < n_tiles)
    def _():
        body(hr, foxr, retr, wo, g2, wg, wu, wd, gf, outr, hn_scr, act_scr)

    if not final:
        @pl.when(t == n_tiles)
        def _():
            body(hm, foxm, retm, wo, g2, wg, wu, wd, gf, outm, hn_scr, act_scr)


def _outffn(h_r, fox_r, ret_r, h_m, fox_m, ret_m, wo, g2, wg, wu, wd, gf, *, layer, final):
    n_r, d = h_r.shape
    n_m = h_m.shape[0]
    d_ff = wg.shape[2]
    tm = _row_tile(n_r, OUTFFN_ROWS)
    n_tiles = n_r // tm
    assert n_m <= tm

    def real(cols):
        return pl.BlockSpec((tm, cols), lambda t: (jnp.minimum(t, n_tiles - 1), 0))

    def meta(cols):
        return pl.BlockSpec((n_m, cols), lambda t: (0, 0))

    weights = [_layer_spec(wo.shape, layer), _const_spec(g2.shape), _layer_spec(wg.shape, layer),
               _layer_spec(wu.shape, layer), _layer_spec(wd.shape, layer), _const_spec(gf.shape)]
    real_in = [real(d), real(fox_r.shape[1]), real(ret_r.shape[1])]
    kern = functools.partial(_outffn_kernel, n_tiles=n_tiles, final=final)
    scratch = [pltpu.VMEM((tm, d), BF16),
               pltpu.VMEM((tm, d_ff), BF16)]
    if final:
        return pl.pallas_call(
            kern, grid=(n_tiles,), in_specs=real_in + weights, out_specs=real(d),
            out_shape=jax.ShapeDtypeStruct((n_r, d), F32), scratch_shapes=scratch,
            compiler_params=_params(1), name="outffn_last",
        )(h_r, fox_r, ret_r, wo, g2, wg, wu, wd, gf), None
    meta_in = [meta(d), meta(fox_m.shape[1]), meta(ret_m.shape[1])]
    return pl.pallas_call(
        kern, grid=(n_tiles + 1,), in_specs=real_in + meta_in + weights,
        out_specs=[real(d), meta(d)],
        out_shape=[jax.ShapeDtypeStruct((n_r, d), F32), jax.ShapeDtypeStruct((n_m, d), F32)],
        scratch_shapes=scratch, compiler_params=_params(1), name="outffn",
    )(h_r, fox_r, ret_r, h_m, fox_m, ret_m, wo, g2, wg, wu, wd, gf)


def kernel(x, meta_tokens, attn_norm, w_in, b_fgate, ret_gn, w_out, ffn_norm,
           w_gate, w_up, w_down, final_norm):
    batch, seq, d = x.shape
    depth = w_in.shape[0]
    fox_heads = b_fgate.shape[1]
    fox_w = fox_heads * FOX_HEAD_DIM
    ret_w = RET_HEADS * RET_HEAD_DIM
    d_ff = w_gate.shape[2]
    assert w_in.shape[2] == 3 * fox_w + fox_heads + 4 * ret_w
    assert fox_w % LANES == 0 and fox_heads <= LANES and d_ff % FF_CHUNK == 0
    assert seq % FOX_TILE == 0 and seq % RET_CHUNK == 0

    h_r = x.reshape(batch * seq, d)
    h_m = jnp.broadcast_to(meta_tokens[None].astype(x.dtype), (batch, N_META, d)).reshape(batch * N_META, d)

    inv_freq = ROPE_BASE ** (-jnp.arange(0, RET_HEAD_DIM, 2, dtype=F32) / RET_HEAD_DIM)
    ang = jnp.arange(seq + N_META, dtype=F32)[:, None] * inv_freq[None, :]
    cos, sin = jnp.cos(ang), jnp.sin(ang)
    cs = jnp.concatenate([cos, cos], axis=1)
    sn = jnp.concatenate([-sin, sin], axis=1)
    cs_r, sn_r = cs[N_META:], sn[N_META:]
    cs_m, sn_m = jnp.tile(cs[:N_META], (batch, 1)), jnp.tile(sn[:N_META], (batch, 1))

    split = 3 * fox_w
    w_all = jnp.concatenate(
        [w_in[..., :split], w_in[..., split + fox_heads:], w_in[..., split:split + fox_heads],
         jnp.zeros((depth, d, LANES - fox_heads), w_in.dtype)], axis=-1).astype(BF16)
    wo, wg, wu, wd = (w.astype(BF16) for w in (w_out, w_gate, w_up, w_down))
    sel = jnp.asarray(_bias_selector(fox_heads), BF16)
    gf = final_norm.reshape(1, d)

    for i in range(depth):
        bl = jnp.pad(b_fgate[i], (0, LANES - fox_heads)).reshape(1, LANES)
        fox_r, ret_r, logf_r, kt_r, fox_m, ret_m, logf_m = _inproj(
            h_r, h_m, cs_r, sn_r, cs_m, sn_m, attn_norm[i].reshape(1, d), w_all, bl,
            layer=i, seq=seq, fox_w=fox_w, ret_w=ret_w)
        qaug, kaug, qaug_m, kaug_m = _decay(logf_r, logf_m, sel, batch=batch, n_heads=fox_heads)
        fo_r, fo_m = _fox(fox_r, fox_m, qaug, kaug, qaug_m, kaug_m, batch=batch)
        ro_r, ro_m = _retention(ret_r, kt_r, ret_m, ret_gn[i].reshape(1, ret_w), batch=batch)
        h_r, h_m = _outffn(h_r, fo_r.reshape(batch * seq, fox_w), ro_r.reshape(batch * seq, ret_w),
                           h_m, fo_m.reshape(batch * N_META, fox_w), ro_m.reshape(batch * N_META, ret_w),
                           wo, ffn_norm[i].reshape(1, d), wg, wu, wd, gf,
                           layer=i, final=(i == depth - 1))
    return h_r.reshape(batch, seq, d)
```

```python
import functools
import math

import numpy as np
import jax
import jax.numpy as jnp
from jax import lax
from jax.experimental import pallas as pl
from jax.experimental.pallas import tpu as pltpu

N_META = 16
FOX_HEAD_DIM = 64
RET_HEADS = 4
RET_HEAD_DIM = 128
RET_CHUNK = 128
ROPE_BASE = 10000.0
EPS = 1e-6
RET_LOG_GAMMA = tuple(math.log(1.0 - 2.0 ** (-5 - h)) for h in range(RET_HEADS))
LOG2E = math.log2(math.e)

LANES = 128
BF16_SUBLANES = 16
VMEM_LIMIT_BYTES = 56 * 1024 * 1024

N_SPLIT = 3
BIAS_LANES = 2 * N_SPLIT

INPROJ_ROWS = 1024
OUTFFN_ROWS = 512
FF_CHUNK = 256
FOX_TILE = 256
FOX_SCORE_ROWS = 128
FOX_SLOTS = 6

F32 = jnp.float32
BF16 = jnp.bfloat16

assert RET_CHUNK == RET_HEAD_DIM == LANES


def _dot(a, b):
    return jnp.dot(a, b, preferred_element_type=F32)


def _dot_nt(a, b):
    return lax.dot_general(a, b, (((1,), (1,)), ((), ())), preferred_element_type=F32)


def _dot_tn(a, b):
    return lax.dot_general(a, b, (((0,), (0,)), ((), ())), preferred_element_type=F32)


def _rmsnorm(x, g):
    return x * lax.rsqrt(jnp.mean(x * x, axis=-1, keepdims=True) + EPS) * g


def _row_tile(n, cap):
    best = None
    for t in range(BF16_SUBLANES, min(n, cap) + 1, BF16_SUBLANES):
        if n % t == 0:
            best = t
    assert best is not None, n
    return best


def _const_spec(shape):
    nd = len(shape)
    return pl.BlockSpec(shape, lambda *_: (0,) * nd, pipeline_mode=pl.Buffered(1))


def _layer_spec(shape, layer):
    nd = len(shape) - 1
    return pl.BlockSpec((None,) + tuple(shape[1:]), lambda *_: (layer,) + (0,) * nd,
                        pipeline_mode=pl.Buffered(1))


def _params(n_axes):
    return pltpu.CompilerParams(dimension_semantics=("arbitrary",) * n_axes,
                                vmem_limit_bytes=VMEM_LIMIT_BYTES)


def _inproj_body(h_ref, cs_ref, sn_ref, g_ref, w_ref, bl_ref, fox_ref, ret_ref, logf_ref, kt_ref=None,
                 *, fox_w, ret_w):
    ret_col = {0: 0, 1: ret_w, 2: 2 * ret_w, 3: 3 * ret_w} if kt_ref is None else {0: 0, 2: ret_w, 3: 2 * ret_w}
    rows = h_ref.shape[0]
    halves = [slice(0, rows // 2), slice(rows // 2, rows)]
    xns = [_rmsnorm(h_ref[r, :], g_ref[...]).astype(BF16) for r in halves]
    q_scale = FOX_HEAD_DIM ** -0.5 * LOG2E
    k_scale = RET_HEAD_DIM ** -0.5
    for c0 in range(0, 3 * fox_w, fox_w):
        for r, xn in zip(halves, xns):
            y = _dot(xn, w_ref[:, c0:c0 + fox_w])
            if c0 == 0:
                y = y * q_scale
            fox_ref[r, c0:c0 + fox_w] = y.astype(BF16)
    base = 3 * fox_w
    for part in range(4):
        for r, xn in zip(halves, xns):
            y = _dot(xn, w_ref[:, base + part * ret_w:base + (part + 1) * ret_w])
            if part < 2:
                cs = cs_ref[r, :]
                sn = sn_ref[r, :]
                for hh in range(RET_HEADS):
                    sl = slice(hh * RET_HEAD_DIM, (hh + 1) * RET_HEAD_DIM)
                    yh = y[:, sl]
                    yh = yh * cs + pltpu.roll(yh, RET_HEAD_DIM // 2, axis=1) * sn
                    if part == 1:
                        yh = yh * k_scale
                    if part == 1 and kt_ref is not None:
                        kt_ref[sl, r] = yh.astype(BF16).T
                    else:
                        ret_ref[r, ret_col[part] + sl.start:ret_col[part] + sl.stop] = yh.astype(BF16)
            else:
                ret_ref[r, ret_col[part]:ret_col[part] + ret_w] = y.astype(BF16)
    for r, xn in zip(halves, xns):
        z = _dot(xn, w_ref[:, base + 4 * ret_w:base + 4 * ret_w + LANES]) + bl_ref[...]
        logf_ref[r, :] = jnp.minimum(z, 0.0) - jnp.log(1.0 + jnp.exp(-jnp.abs(z)))


def _inproj_kernel(hr_ref, hm_ref, csr_ref, snr_ref, csm_ref, snm_ref, g_ref, w_ref, bl_ref,
                   foxr_ref, retr_ref, logfr_ref, ktr_ref, foxm_ref, retm_ref, logfm_ref,
                   *, n_tiles, fox_w, ret_w):
    body = functools.partial(_inproj_body, fox_w=fox_w, ret_w=ret_w)
    t = pl.program_id(0)

    @pl.when(t < n_tiles)
    def _():
        body(hr_ref, csr_ref, snr_ref, g_ref, w_ref, bl_ref, foxr_ref, retr_ref, logfr_ref, ktr_ref)

    @pl.when(t == n_tiles)
    def _():
        body(hm_ref, csm_ref, snm_ref, g_ref, w_ref, bl_ref, foxm_ref, retm_ref, logfm_ref)


def _inproj(h_r, h_m, cs_r, sn_r, cs_m, sn_m, g, w_all, bl, *, layer, seq, fox_w, ret_w):
    n_r, d = h_r.shape
    n_m = h_m.shape[0]
    tm = _row_tile(seq, INPROJ_ROWS)
    n_tiles = n_r // tm
    per_seq = seq // tm
    fox_cols, ret_cols = 3 * fox_w, 4 * ret_w

    def real(cols):
        return pl.BlockSpec((tm, cols), lambda t: (jnp.minimum(t, n_tiles - 1), 0))

    def table():
        return pl.BlockSpec((tm, LANES), lambda t: (jnp.minimum(t, n_tiles - 1) % per_seq, 0))

    def meta(cols):
        return pl.BlockSpec((n_m, cols), lambda t: (0, 0))

    def transposed_keys():
        return pl.BlockSpec((None, ret_w, tm), lambda t: (jnp.minimum(t, n_tiles - 1) // per_seq, 0,
                                                           jnp.minimum(t, n_tiles - 1) % per_seq))

    kern = functools.partial(_inproj_kernel, n_tiles=n_tiles, fox_w=fox_w, ret_w=ret_w)
    return pl.pallas_call(
        kern,
        grid=(n_tiles + 1,),
        in_specs=[real(d), meta(d), table(), table(), meta(LANES), meta(LANES),
                  _const_spec((1, d)), _layer_spec(w_all.shape, layer), _const_spec((1, LANES))],
        out_specs=[real(fox_cols), real(ret_cols - ret_w), real(LANES), transposed_keys(),
                   meta(fox_cols), meta(ret_cols), meta(LANES)],
        out_shape=[
            jax.ShapeDtypeStruct((n_r, fox_cols), BF16),
            jax.ShapeDtypeStruct((n_r, ret_cols - ret_w), BF16),
            jax.ShapeDtypeStruct((n_r, LANES), F32),
            jax.ShapeDtypeStruct((n_r // seq, ret_w, seq), BF16),
            jax.ShapeDtypeStruct((n_m, fox_cols), BF16),
            jax.ShapeDtypeStruct((n_m, ret_cols), BF16),
            jax.ShapeDtypeStruct((n_m, LANES), F32),
        ],
        compiler_params=_params(1),
        name="inproj",
    )(h_r, h_m, cs_r, sn_r, cs_m, sn_m, g, w_all, bl)


def _bias_lane(pair, a):
    return (FOX_HEAD_DIM if a == 0 else 0) + BIAS_LANES * pair


def _split3(x):
    hi = x.astype(BF16)
    r1 = x - hi.astype(F32)
    mid = r1.astype(BF16)
    lo = (r1 - mid.astype(F32)).astype(BF16)
    return hi, mid, lo


def _bias_selector(n_heads):
    sel = np.zeros((N_SPLIT * LANES, LANES), np.float32)
    for head in range(n_heads):
        lane0 = _bias_lane(head // 2, head % 2)
        for p in range(N_SPLIT):
            sel[p * LANES + head, lane0 + p] = 1.0
            sel[p * LANES + head, lane0 + N_SPLIT + p] = -1.0
    return sel


def _decay_kernel(logf_ref, logfm_ref, sel_ref, qaug_ref, kaug_ref, qaugm_ref, kaugm_ref, c_scr,
                  *, n_heads):
    blk = LANES
    seq = logf_ref.shape[0]
    ri = lax.broadcasted_iota(jnp.int32, (blk, blk), 0)
    ci = lax.broadcasted_iota(jnp.int32, (blk, blk), 1)
    tri = (ci <= ri).astype(BF16)

    def prefix(x):
        hi, mid, lo = _split3(x)
        return _dot(tri, hi) + _dot(tri, mid) + _dot(tri, lo)

    cb = prefix(jnp.concatenate([logfm_ref[...], jnp.zeros((blk - N_META, LANES), F32)], axis=0))
    c_scr[0:blk, :] = cb
    carry = cb[N_META - 1:N_META, :]
    for r0 in range(0, seq, blk):
        cb = prefix(logf_ref[r0:r0 + blk, :]) + carry
        c_scr[blk + r0:2 * blk + r0, :] = cb
        carry = cb[blk - 1:blk, :]

    in_half = lax.broadcasted_iota(jnp.int32, (1, LANES), 1) % FOX_HEAD_DIM
    used = in_half < BIAS_LANES * (n_heads // 2)
    first = used & (in_half % BIAS_LANES < N_SPLIT)
    second = used & (in_half % BIAS_LANES >= N_SPLIT)

    def route(c, q_out, k_out):
        hi, mid, lo = _split3(c * LOG2E)
        routed = _dot(jnp.concatenate([hi, mid, lo], axis=1), sel_ref[...])
        q_out[...] = jnp.where(first, routed, jnp.where(second, 1.0, 0.0)).astype(BF16)
        k_out[...] = jnp.where(second, routed, jnp.where(first, 1.0, 0.0)).astype(BF16)

    route(c_scr[0:N_META, :], qaugm_ref, kaugm_ref)
    route(c_scr[blk:, :], qaug_ref, kaug_ref)


def _decay(logf_r, logf_m, sel, *, batch, n_heads):
    seq = logf_r.shape[0] // batch

    def rows(n):
        return pl.BlockSpec((None, n, LANES), lambda b: (b, 0, 0))

    return pl.pallas_call(
        functools.partial(_decay_kernel, n_heads=n_heads),
        grid=(batch,),
        in_specs=[rows(seq), rows(N_META), _const_spec(sel.shape)],
        out_specs=[rows(seq), rows(seq), rows(N_META), rows(N_META)],
        out_shape=[jax.ShapeDtypeStruct((batch, seq, LANES), BF16)] * 2
        + [jax.ShapeDtypeStruct((batch, N_META, LANES), BF16)] * 2,
        scratch_shapes=[pltpu.VMEM((LANES + seq, LANES), F32)],
        compiler_params=_params(1),
        name="decay",
    )(logf_r.reshape(batch, seq, LANES), logf_m.reshape(batch, N_META, LANES), sel)


def _fox_kernel(q_ref, k_ref, v_ref, qm_ref, km_ref, vm_ref, qaug_ref, kaug_ref, qaugm_ref, kaugm_ref,
                o_ref, om_ref, k_scr, km_scr, vt_scr, vmt_scr, vm_scr, s_scr, sm_scr, p_scr, pm_scr, qt_scr,
                *, tile, score_rows):
    seq = q_ref.shape[0]
    n_qblk = seq // tile
    pad_rows = LANES - N_META
    pair = pl.program_id(1)
    lane = lax.broadcasted_iota(jnp.int32, (1, LANES), 1)
    own = [lane < FOX_HEAD_DIM, lane >= FOX_HEAD_DIM]
    free0 = [FOX_HEAD_DIM, 0]
    n_slots = s_scr.shape[0]

    def augment(a, x, aug):
        lane0 = free0[a] + BIAS_LANES * pair
        mine = (lane >= lane0) & (lane < lane0 + BIAS_LANES)
        return jnp.where(own[a], x, jnp.where(mine, aug, jnp.zeros_like(aug)))

    pad = jnp.zeros((pad_rows, LANES), BF16)
    for a in (0, 1):
        ones_col = jnp.where(lane == free0[a], 1.0, 0.0).astype(BF16)
        km_scr[a] = jnp.concatenate([augment(a, km_ref[...], kaugm_ref[...]), pad], axis=0)
        v_meta = jnp.concatenate([jnp.where(own[a], vm_ref[...], ones_col), pad], axis=0)
        vm_scr[a] = v_meta
        vmt_scr[a] = v_meta.T
        for r0 in range(0, seq, tile):
            rows = slice(r0, r0 + tile)
            k_scr[a, rows, :] = augment(a, k_ref[rows, :], kaug_ref[rows, :])
            vt_scr[a, :, rows] = jnp.where(own[a], v_ref[rows, :], ones_col).T
    for slot in range(n_slots):
        pm_scr[slot, N_META:LANES, :] = jnp.zeros((pad_rows, tile), BF16)

    def meta_queries():
        row_m = lax.broadcasted_iota(jnp.int32, (N_META, LANES), 0)
        col_m = lax.broadcasted_iota(jnp.int32, (N_META, LANES), 1)
        outs = []
        for a in (0, 1):
            s = _dot_nt(augment(a, qm_ref[...], qaugm_ref[...]), km_scr[a])
            s = jnp.where(col_m <= row_m, s, -jnp.inf)
            p = jnp.exp2(s - jnp.max(s, axis=1, keepdims=True)).astype(BF16)
            acc = _dot(p, vm_scr[a])
            outs.append(acc * (1.0 / acc[:, free0[a]:free0[a] + 1]))
        om_ref[...] = jnp.where(own[0], outs[0], outs[1]).astype(om_ref.dtype)

    def score_stage(i, a):
        slot = (2 * i + a) % n_slots
        d0, kt = i * tile, (i + 1) * tile
        qt_scr[slot] = augment(a, q_ref[d0:kt, :], qaug_ref[d0:kt, :]).T
        qt = qt_scr[slot]
        sm = _dot(km_scr[a, 0:N_META, :], qt)
        sm_scr[slot] = sm
        mx = jnp.max(sm.reshape(N_META // 8, 8, tile), axis=0)
        for r0 in range(0, kt, score_rows):
            rows = min(score_rows, kt - r0)
            s = _dot(k_scr[a, r0:r0 + rows, :], qt)
            if r0 + rows > d0:
                key_i = lax.broadcasted_iota(jnp.int32, (rows, tile), 0) + (r0 - d0)
                qry_i = lax.broadcasted_iota(jnp.int32, (rows, tile), 1)
                s = jnp.where(key_i <= qry_i, s, -jnp.inf)
            s_scr[slot, r0:r0 + rows, :] = s
            mx = jnp.maximum(mx, jnp.max(s.reshape(rows // 8, 8, tile), axis=0))
        return jnp.max(mx, axis=0, keepdims=True)

    def value_stage(i, maxes_i):
        kt = (i + 1) * tile
        slots = [(2 * i + a) % n_slots for a in (0, 1)]
        for a in (0, 1):
            pm_scr[slots[a], 0:N_META, :] = jnp.exp2(sm_scr[slots[a]] - maxes_i[a]).astype(BF16)
        for r0 in range(0, kt, LANES):
            for a in (0, 1):
                p_scr[slots[a], r0:r0 + LANES, :] = jnp.exp2(
                    s_scr[slots[a], r0:r0 + LANES, :] - maxes_i[a]).astype(BF16)
        outs = []
        for a in (0, 1):
            acc_t = (_dot(vt_scr[a, :, 0:kt], p_scr[slots[a], 0:kt, :])
                     + _dot(vmt_scr[a], pm_scr[slots[a]]))
            l = acc_t[free0[a]:free0[a] + 1, :]
            outs.append((acc_t * (1.0 / l)).T)
        return outs

    ahead = n_slots // 2 - 1
    maxes = {j: [score_stage(j, a) for a in (0, 1)] for j in range(min(ahead, n_qblk))}
    for i in range(n_qblk):
        if i + ahead < n_qblk:
            maxes[i + ahead] = [score_stage(i + ahead, a) for a in (0, 1)]
        outs = value_stage(i, maxes[i])
        o_ref[i * tile:(i + 1) * tile, :] = jnp.where(own[0], outs[0], outs[1]).astype(o_ref.dtype)
        if i == 0:
            meta_queries()


def _fox(fox_r, fox_m, qaug, kaug, qaug_m, kaug_m, *, batch):
    n_r, cols = fox_r.shape
    seq = n_r // batch
    width = cols // 3
    n_pairs = width // LANES
    tile = FOX_TILE
    assert seq % tile == 0 and BIAS_LANES * n_pairs <= FOX_HEAD_DIM
    qkv_r = fox_r.reshape(batch, seq, cols)
    qkv_m = fox_m.reshape(batch, N_META, cols)
    kern = functools.partial(_fox_kernel, tile=tile, score_rows=FOX_SCORE_ROWS)

    def cols_of(rows, j0):
        return pl.BlockSpec((None, rows, LANES), lambda b, p: (b, 0, j0 + p))

    def shared(rows):
        return pl.BlockSpec((None, rows, LANES), lambda b, p: (b, 0, 0))

    return pl.pallas_call(
        kern,
        grid=(batch, n_pairs),
        in_specs=[cols_of(seq, 0), cols_of(seq, n_pairs), cols_of(seq, 2 * n_pairs),
                  cols_of(N_META, 0), cols_of(N_META, n_pairs), cols_of(N_META, 2 * n_pairs),
                  shared(seq), shared(seq), shared(N_META), shared(N_META)],
        out_specs=[cols_of(seq, 0), cols_of(N_META, 0)],
        out_shape=[jax.ShapeDtypeStruct((batch, seq, width), BF16),
                   jax.ShapeDtypeStruct((batch, N_META, width), BF16)],
        scratch_shapes=[
            pltpu.VMEM((2, seq, LANES), BF16),
            pltpu.VMEM((2, LANES, LANES), BF16),
            pltpu.VMEM((2, LANES, seq), BF16),
            pltpu.VMEM((2, LANES, LANES), BF16),
            pltpu.VMEM((2, LANES, LANES), BF16),
            pltpu.VMEM((FOX_SLOTS, seq, tile), F32),
            pltpu.VMEM((FOX_SLOTS, N_META, tile), F32),
            pltpu.VMEM((FOX_SLOTS, seq, tile), BF16),
            pltpu.VMEM((FOX_SLOTS, LANES, tile), BF16),
            pltpu.VMEM((FOX_SLOTS, LANES, tile), BF16),
        ],
        compiler_params=_params(2),
        name="fox",
    )(qkv_r, qkv_r, qkv_r, qkv_m, qkv_m, qkv_m, qaug, kaug, qaug_m, kaug_m)


def _ret_kernel(q_ref, kt_ref, v_ref, g_ref, qm_ref, km_ref, vm_ref, gm_ref, gn_ref,
                o_ref, om_ref, state_scr, dec_scr):
    C = RET_CHUNK
    n_chunks = q_ref.shape[0] // C
    ri = lax.broadcasted_iota(jnp.int32, (C, C), 0)
    ci = lax.broadcasted_iota(jnp.int32, (C, C), 1)
    diff = ri - ci
    row_f = ri.astype(F32)

    for hh in range(RET_HEADS):
        log_g = RET_LOG_GAMMA[hh]
        dec_scr[hh, 0] = jnp.where(diff >= 0, jnp.exp(jnp.maximum(diff, 0).astype(F32) * log_g), 0.0)
        dec_scr[hh, 1] = jnp.exp((row_f + 1.0) * log_g)
        dec_scr[hh, 2] = jnp.exp((C - 1.0 - ci.astype(F32)) * log_g)
        dec_scr[hh, 3] = jnp.exp((N_META - 1.0 - row_f) * log_g)

    def head(hh):
        return slice(hh * RET_HEAD_DIM, (hh + 1) * RET_HEAD_DIM)

    def finish(o, g, out_ref, rows, hh):
        mu = jnp.mean(o, axis=-1, keepdims=True)
        d = o - mu
        var = jnp.mean(d * d, axis=-1, keepdims=True)
        y = d * lax.rsqrt(var + EPS) * gn_ref[:, head(hh)]
        g = g.astype(F32)
        out_ref[rows, head(hh)] = (g * (1.0 / (1.0 + jnp.exp(-g))) * y).astype(out_ref.dtype)

    def intra(q, k, v, hh):
        s = _dot_nt(q, k) * dec_scr[hh, 0]
        return _dot(s.astype(BF16), v)

    def decayed_keys(k, hh, table):
        return (k.astype(F32) * dec_scr[hh, table]).astype(BF16)

    pad = jnp.zeros((C - N_META, RET_HEAD_DIM), BF16)
    for hh in range(RET_HEADS):
        q, k, v = (jnp.concatenate([r[:, head(hh)], pad], axis=0) for r in (qm_ref, km_ref, vm_ref))
        finish(intra(q, k, v, hh)[0:N_META, :], gm_ref[:, head(hh)], om_ref, slice(0, N_META), hh)
        state_scr[hh] = _dot_tn(decayed_keys(k, hh, 3), v)

    def rows_of(c):
        return slice(c * C, (c + 1) * C)

    def scores(c):
        out = []
        for hh in range(RET_HEADS):
            q, v = q_ref[rows_of(c), head(hh)], v_ref[rows_of(c), head(hh)]
            k_t = kt_ref[head(hh), rows_of(c)]
            out.append((_dot(q, k_t), _dot(decayed_keys(k_t, hh, 2), v)))
        return out

    def carry(c, kv):
        out = []
        for hh in range(RET_HEADS):
            state = state_scr[hh]
            out.append(_dot(q_ref[rows_of(c), head(hh)], state.astype(BF16)) * dec_scr[hh, 1])
            chunk_dec = jnp.exp(jnp.full((1, 1), C * RET_LOG_GAMMA[hh], F32))
            state_scr[hh] = chunk_dec * state + kv[hh]
        return out

    def outputs(c, s, inter):
        for hh in range(RET_HEADS):
            o = _dot((s[hh] * dec_scr[hh, 0]).astype(BF16), v_ref[rows_of(c), head(hh)]) + inter[hh]
            finish(o, g_ref[rows_of(c), head(hh)], o_ref, rows_of(c), hh)

    cur = scores(0)
    inter = carry(0, [kv for _, kv in cur])
    for c in range(n_chunks):
        nxt = scores(c + 1) if c + 1 < n_chunks else None
        outputs(c, [s for s, _ in cur], inter)
        if nxt is not None:
            inter = carry(c + 1, [kv for _, kv in nxt])
        cur = nxt


def _retention(ret_r, kt_r, ret_m, gn, *, batch):
    n_r, cols = ret_r.shape
    seq = n_r // batch
    width = cols // 3
    x_r = ret_r.reshape(batch, seq, cols)
    x_m = ret_m.reshape(batch, N_META, 4 * width)

    def part(rows, j):
        return pl.BlockSpec((None, rows, width), lambda b: (b, 0, j))

    return pl.pallas_call(
        _ret_kernel,
        grid=(batch,),
        in_specs=[part(seq, 0), pl.BlockSpec((None, width, seq), lambda b: (b, 0, 0)), part(seq, 1), part(seq, 2)]
        + [part(N_META, j) for j in range(4)] + [_const_spec(gn.shape)],
        out_specs=[part(seq, 0), part(N_META, 0)],
        out_shape=[jax.ShapeDtypeStruct((batch, seq, width), BF16),
                   jax.ShapeDtypeStruct((batch, N_META, width), BF16)],
        scratch_shapes=[pltpu.VMEM((RET_HEADS, RET_HEAD_DIM, RET_HEAD_DIM), F32),
                        pltpu.VMEM((RET_HEADS, 4, RET_CHUNK, RET_HEAD_DIM), F32)],
        compiler_params=_params(1),
        name="retention",
    )(x_r, kt_r, x_r, x_r, x_m, x_m, x_m, x_m, gn)


def _outffn_body(h_ref, fox_ref, ret_ref, wo_ref, g2_ref, wg_ref, wu_ref, wd_ref, gf_ref, o_ref,
                 hn_scr, act_scr, *, final):
    rows = h_ref.shape[0]
    fox_w = fox_ref.shape[1]
    halves = [slice(0, rows // 2), slice(rows // 2, rows)]
    for r in halves:
        h1 = h_ref[r, :] + _dot(fox_ref[r, :], wo_ref[0:fox_w, :]) + _dot(ret_ref[r, :], wo_ref[fox_w:, :])
        hn_scr[r, :] = _rmsnorm(h1, g2_ref[...]).astype(BF16)
        o_ref[r, :] = h1
    for c0 in range(0, wg_ref.shape[1], FF_CHUNK):
        for r in halves:
            hn = hn_scr[r, :]
            gate = _dot(hn, wg_ref[:, c0:c0 + FF_CHUNK])
            up = _dot(hn, wu_ref[:, c0:c0 + FF_CHUNK])
            act_scr[r, c0:c0 + FF_CHUNK] = (gate * (1.0 / (1.0 + jnp.exp(-gate))) * up).astype(BF16)
    for r in halves:
        out = o_ref[r, :] + _dot(act_scr[r, :], wd_ref[...])
        if final:
            out = _rmsnorm(out, gf_ref[...])
        o_ref[r, :] = out


def _outffn_kernel(*refs, n_tiles, final):
    if final:
        (hr, foxr, retr, wo, g2, wg, wu, wd, gf, outr, hn_scr, act_scr) = refs
    else:
        (hr, foxr, retr, hm, foxm, retm, wo, g2, wg, wu, wd, gf, outr, outm, hn_scr, act_scr) = refs
    body = functools.partial(_outffn_body, final=final)
    t = pl.program_id(0)

    @pl.when(t < n_tiles)
    def _():
        body(hr, foxr, retr, wo, g2, wg, wu, wd, gf, outr, hn_scr, act_scr)

    if not final:
        @pl.when(t == n_tiles)
        def _():
            body(hm, foxm, retm, wo, g2, wg, wu, wd, gf, outm, hn_scr, act_scr)


def _outffn(h_r, fox_r, ret_r, h_m, fox_m, ret_m, wo, g2, wg, wu, wd, gf, *, layer, final):
    n_r, d = h_r.shape
    n_m = h_m.shape[0]
    d_ff = wg.shape[2]
    tm = _row_tile(n_r, OUTFFN_ROWS)
    n_tiles = n_r // tm
    assert n_m <= tm

    def real(cols):
        return pl.BlockSpec((tm, cols), lambda t: (jnp.minimum(t, n_tiles - 1), 0))

    def meta(cols):
        return pl.BlockSpec((n_m, cols), lambda t: (0, 0))

    weights = [_layer_spec(wo.shape, layer), _const_spec(g2.shape), _layer_spec(wg.shape, layer),
               _layer_spec(wu.shape, layer), _layer_spec(wd.shape, layer), _const_spec(gf.shape)]
    real_in = [real(d), real(fox_r.shape[1]), real(ret_r.shape[1])]
    kern = functools.partial(_outffn_kernel, n_tiles=n_tiles, final=final)
    scratch = [pltpu.VMEM((tm, d), BF16),
               pltpu.VMEM((tm, d_ff), BF16)]
    if final:
        return pl.pallas_call(
            kern, grid=(n_tiles,), in_specs=real_in + weights, out_specs=real(d),
            out_shape=jax.ShapeDtypeStruct((n_r, d), F32), scratch_shapes=scratch,
            compiler_params=_params(1), name="outffn_last",
        )(h_r, fox_r, ret_r, wo, g2, wg, wu, wd, gf), None
    meta_in = [meta(d), meta(fox_m.shape[1]), meta(ret_m.shape[1])]
    return pl.pallas_call(
        kern, grid=(n_tiles + 1,), in_specs=real_in + meta_in + weights,
        out_specs=[real(d), meta(d)],
        out_shape=[jax.ShapeDtypeStruct((n_r, d), F32), jax.ShapeDtypeStruct((n_m, d), F32)],
        scratch_shapes=scratch, compiler_params=_params(1), name="outffn",
    )(h_r, fox_r, ret_r, h_m, fox_m, ret_m, wo, g2, wg, wu, wd, gf)


def kernel(x, meta_tokens, attn_norm, w_in, b_fgate, ret_gn, w_out, ffn_norm,
           w_gate, w_up, w_down, final_norm):
    batch, seq, d = x.shape
    depth = w_in.shape[0]
    fox_heads = b_fgate.shape[1]
    fox_w = fox_heads * FOX_HEAD_DIM
    ret_w = RET_HEADS * RET_HEAD_DIM
    d_ff = w_gate.shape[2]
    assert w_in.shape[2] == 3 * fox_w + fox_heads + 4 * ret_w
    assert fox_w % LANES == 0 and fox_heads <= LANES and d_ff % FF_CHUNK == 0
    assert seq % FOX_TILE == 0 and seq % RET_CHUNK == 0

    h_r = x.reshape(batch * seq, d)
    h_m = jnp.broadcast_to(meta_tokens[None].astype(x.dtype), (batch, N_META, d)).reshape(batch * N_META, d)

    inv_freq = ROPE_BASE ** (-jnp.arange(0, RET_HEAD_DIM, 2, dtype=F32) / RET_HEAD_DIM)
    ang = jnp.arange(seq + N_META, dtype=F32)[:, None] * inv_freq[None, :]
    cos, sin = jnp.cos(ang), jnp.sin(ang)
    cs = jnp.concatenate([cos, cos], axis=1)
    sn = jnp.concatenate([-sin, sin], axis=1)
    cs_r, sn_r = cs[N_META:], sn[N_META:]
    cs_m, sn_m = jnp.tile(cs[:N_META], (batch, 1)), jnp.tile(sn[:N_META], (batch, 1))

    split = 3 * fox_w
    w_all = jnp.concatenate(
        [w_in[..., :split], w_in[..., split + fox_heads:], w_in[..., split:split + fox_heads],
         jnp.zeros((depth, d, LANES - fox_heads), w_in.dtype)], axis=-1).astype(BF16)
    wo, wg, wu, wd = (w.astype(BF16) for w in (w_out, w_gate, w_up, w_down))
    sel = jnp.asarray(_bias_selector(fox_heads), BF16)
    gf = final_norm.reshape(1, d)

    for i in range(depth):
        bl = jnp.pad(b_fgate[i], (0, LANES - fox_heads)).reshape(1, LANES)
        fox_r, ret_r, logf_r, kt_r, fox_m, ret_m, logf_m = _inproj(
            h_r, h_m, cs_r, sn_r, cs_m, sn_m, attn_norm[i].reshape(1, d), w_all, bl,
            layer=i, seq=seq, fox_w=fox_w, ret_w=ret_w)
        qaug, kaug, qaug_m, kaug_m = _decay(logf_r, logf_m, sel, batch=batch, n_heads=fox_heads)
        fo_r, fo_m = _fox(fox_r, fox_m, qaug, kaug, qaug_m, kaug_m, batch=batch)
        ro_r, ro_m = _retention(ret_r, kt_r, ret_m, ret_gn[i].reshape(1, ret_w), batch=batch)
        h_r, h_m = _outffn(h_r, fo_r.reshape(batch * seq, fox_w), ro_r.reshape(batch * seq, ret_w),
                           h_m, fo_m.reshape(batch * N_META, fox_w), ro_m.reshape(batch * N_META, ret_w),
                           wo, ffn_norm[i].reshape(1, d), wg, wu, wd, gf,
                           layer=i, final=(i == depth - 1))
    return h_r.reshape(batch, seq, d)
```

```python
import functools
import math

import numpy as np
import jax
import jax.numpy as jnp
from jax import lax
from jax.experimental import pallas as pl
from jax.experimental.pallas import tpu as pltpu

N_META = 16
FOX_HEAD_DIM = 64
RET_HEADS = 4
RET_HEAD_DIM = 128
RET_CHUNK = 128
ROPE_BASE = 10000.0
EPS = 1e-6
RET_LOG_GAMMA = tuple(math.log(1.0 - 2.0 ** (-5 - h)) for h in range(RET_HEADS))
LOG2E = math.log2(math.e)

LANES = 128
BF16_SUBLANES = 16
VMEM_LIMIT_BYTES = 56 * 1024 * 1024

N_SPLIT = 3
BIAS_LANES = 2 * N_SPLIT

INPROJ_ROWS = 1024
OUTFFN_ROWS = 1024
FF_CHUNK = 256
FOX_TILE = 256
FOX_SCORE_ROWS = 128
FOX_SLOTS = 6

F32 = jnp.float32
BF16 = jnp.bfloat16

assert RET_CHUNK == RET_HEAD_DIM == LANES


def _dot(a, b):
    return jnp.dot(a, b, preferred_element_type=F32)


def _dot_nt(a, b):
    return lax.dot_general(a, b, (((1,), (1,)), ((), ())), preferred_element_type=F32)


def _dot_tn(a, b):
    return lax.dot_general(a, b, (((0,), (0,)), ((), ())), preferred_element_type=F32)


def _rmsnorm(x, g):
    return x * lax.rsqrt(jnp.mean(x * x, axis=-1, keepdims=True) + EPS) * g


def _row_tile(n, cap):
    best = None
    for t in range(BF16_SUBLANES, min(n, cap) + 1, BF16_SUBLANES):
        if n % t == 0:
            best = t
    assert best is not None, n
    return best


def _const_spec(shape):
    nd = len(shape)
    return pl.BlockSpec(shape, lambda *_: (0,) * nd, pipeline_mode=pl.Buffered(1))


def _layer_spec(shape, layer):
    nd = len(shape) - 1
    return pl.BlockSpec((None,) + tuple(shape[1:]), lambda *_: (layer,) + (0,) * nd,
                        pipeline_mode=pl.Buffered(1))


def _params(n_axes):
    return pltpu.CompilerParams(dimension_semantics=("arbitrary",) * n_axes,
                                vmem_limit_bytes=VMEM_LIMIT_BYTES)


def _inproj_body(h_ref, cs_ref, sn_ref, g_ref, w_ref, bl_ref, fox_ref, ret_ref, logf_ref, kt_ref=None,
                 *, fox_w, ret_w):
    ret_col = {0: 0, 1: ret_w, 2: 2 * ret_w, 3: 3 * ret_w} if kt_ref is None else {0: 0, 2: ret_w, 3: 2 * ret_w}
    rows = h_ref.shape[0]
    halves = [slice(0, rows // 2), slice(rows // 2, rows)]
    xns = [_rmsnorm(h_ref[r, :], g_ref[...]).astype(BF16) for r in halves]
    q_scale = FOX_HEAD_DIM ** -0.5 * LOG2E
    k_scale = RET_HEAD_DIM ** -0.5
    for c0 in range(0, 3 * fox_w, fox_w):
        for r, xn in zip(halves, xns):
            y = _dot(xn, w_ref[:, c0:c0 + fox_w])
            if c0 == 0:
                y = y * q_scale
            fox_ref[r, c0:c0 + fox_w] = y.astype(BF16)
    base = 3 * fox_w
    for part in range(4):
        for r, xn in zip(halves, xns):
            y = _dot(xn, w_ref[:, base + part * ret_w:base + (part + 1) * ret_w])
            if part < 2:
                cs = cs_ref[r, :]
                sn = sn_ref[r, :]
                for hh in range(RET_HEADS):
                    sl = slice(hh * RET_HEAD_DIM, (hh + 1) * RET_HEAD_DIM)
                    yh = y[:, sl]
                    yh = yh * cs + pltpu.roll(yh, RET_HEAD_DIM // 2, axis=1) * sn
                    if part == 1:
                        yh = yh * k_scale
                    if part == 1 and kt_ref is not None:
                        kt_ref[sl, r] = yh.astype(BF16).T
                    else:
                        ret_ref[r, ret_col[part] + sl.start:ret_col[part] + sl.stop] = yh.astype(BF16)
            else:
                ret_ref[r, ret_col[part]:ret_col[part] + ret_w] = y.astype(BF16)
    for r, xn in zip(halves, xns):
        z = _dot(xn, w_ref[:, base + 4 * ret_w:base + 4 * ret_w + LANES]) + bl_ref[...]
        logf_ref[r, :] = jnp.minimum(z, 0.0) - jnp.log(1.0 + jnp.exp(-jnp.abs(z)))


def _inproj_kernel(hr_ref, hm_ref, csr_ref, snr_ref, csm_ref, snm_ref, g_ref, w_ref, bl_ref,
                   foxr_ref, retr_ref, logfr_ref, ktr_ref, foxm_ref, retm_ref, logfm_ref,
                   *, n_tiles, fox_w, ret_w):
    body = functools.partial(_inproj_body, fox_w=fox_w, ret_w=ret_w)
    t = pl.program_id(0)

    @pl.when(t < n_tiles)
    def _():
        body(hr_ref, csr_ref, snr_ref, g_ref, w_ref, bl_ref, foxr_ref, retr_ref, logfr_ref, ktr_ref)

    @pl.when(t == n_tiles)
    def _():
        body(hm_ref, csm_ref, snm_ref, g_ref, w_ref, bl_ref, foxm_ref, retm_ref, logfm_ref)


def _inproj(h_r, h_m, cs_r, sn_r, cs_m, sn_m, g, w_all, bl, *, layer, seq, fox_w, ret_w):
    n_r, d = h_r.shape
    n_m = h_m.shape[0]
    tm = _row_tile(seq, INPROJ_ROWS)
    n_tiles = n_r // tm
    per_seq = seq // tm
    fox_cols, ret_cols = 3 * fox_w, 4 * ret_w

    def real(cols):
        return pl.BlockSpec((tm, cols), lambda t: (jnp.minimum(t, n_tiles - 1), 0))

    def table():
        return pl.BlockSpec((tm, LANES), lambda t: (jnp.minimum(t, n_tiles - 1) % per_seq, 0))

    def meta(cols):
        return pl.BlockSpec((n_m, cols), lambda t: (0, 0))

    def transposed_keys():
        return pl.BlockSpec((None, ret_w, tm), lambda t: (jnp.minimum(t, n_tiles - 1) // per_seq, 0,
                                                           jnp.minimum(t, n_tiles - 1) % per_seq))

    kern = functools.partial(_inproj_kernel, n_tiles=n_tiles, fox_w=fox_w, ret_w=ret_w)
    return pl.pallas_call(
        kern,
        grid=(n_tiles + 1,),
        in_specs=[real(d), meta(d), table(), table(), meta(LANES), meta(LANES),
                  _const_spec((1, d)), _layer_spec(w_all.shape, layer), _const_spec((1, LANES))],
        out_specs=[real(fox_cols), real(ret_cols - ret_w), real(LANES), transposed_keys(),
                   meta(fox_cols), meta(ret_cols), meta(LANES)],
        out_shape=[
            jax.ShapeDtypeStruct((n_r, fox_cols), BF16),
            jax.ShapeDtypeStruct((n_r, ret_cols - ret_w), BF16),
            jax.ShapeDtypeStruct((n_r, LANES), F32),
            jax.ShapeDtypeStruct((n_r // seq, ret_w, seq), BF16),
            jax.ShapeDtypeStruct((n_m, fox_cols), BF16),
            jax.ShapeDtypeStruct((n_m, ret_cols), BF16),
            jax.ShapeDtypeStruct((n_m, LANES), F32),
        ],
        compiler_params=_params(1),
        name="inproj",
    )(h_r, h_m, cs_r, sn_r, cs_m, sn_m, g, w_all, bl)


def _bias_lane(pair, a):
    return (FOX_HEAD_DIM if a == 0 else 0) + BIAS_LANES * pair


def _split3(x):
    hi = x.astype(BF16)
    r1 = x - hi.astype(F32)
    mid = r1.astype(BF16)
    lo = (r1 - mid.astype(F32)).astype(BF16)
    return hi, mid, lo


def _bias_selector(n_heads):
    sel = np.zeros((N_SPLIT * LANES, LANES), np.float32)
    for head in range(n_heads):
        lane0 = _bias_lane(head // 2, head % 2)
        for p in range(N_SPLIT):
            sel[p * LANES + head, lane0 + p] = 1.0
            sel[p * LANES + head, lane0 + N_SPLIT + p] = -1.0
    return sel


def _decay_kernel(logf_ref, logfm_ref, sel_ref, qaug_ref, kaug_ref, qaugm_ref, kaugm_ref, c_scr,
                  *, n_heads):
    blk = LANES
    seq = logf_ref.shape[0]
    ri = lax.broadcasted_iota(jnp.int32, (blk, blk), 0)
    ci = lax.broadcasted_iota(jnp.int32, (blk, blk), 1)
    tri = (ci <= ri).astype(BF16)

    def prefix(x):
        hi, mid, lo = _split3(x)
        return _dot(tri, hi) + _dot(tri, mid) + _dot(tri, lo)

    cb = prefix(jnp.concatenate([logfm_ref[...], jnp.zeros((blk - N_META, LANES), F32)], axis=0))
    c_scr[0:blk, :] = cb
    carry = cb[N_META - 1:N_META, :]
    for r0 in range(0, seq, blk):
        cb = prefix(logf_ref[r0:r0 + blk, :]) + carry
        c_scr[blk + r0:2 * blk + r0, :] = cb
        carry = cb[blk - 1:blk, :]

    in_half = lax.broadcasted_iota(jnp.int32, (1, LANES), 1) % FOX_HEAD_DIM
    used = in_half < BIAS_LANES * (n_heads // 2)
    first = used & (in_half % BIAS_LANES < N_SPLIT)
    second = used & (in_half % BIAS_LANES >= N_SPLIT)

    def route(c, q_out, k_out):
        hi, mid, lo = _split3(c * LOG2E)
        routed = _dot(jnp.concatenate([hi, mid, lo], axis=1), sel_ref[...])
        q_out[...] = jnp.where(first, routed, jnp.where(second, 1.0, 0.0)).astype(BF16)
        k_out[...] = jnp.where(second, routed, jnp.where(first, 1.0, 0.0)).astype(BF16)

    route(c_scr[0:N_META, :], qaugm_ref, kaugm_ref)
    route(c_scr[blk:, :], qaug_ref, kaug_ref)


def _decay(logf_r, logf_m, sel, *, batch, n_heads):
    seq = logf_r.shape[0] // batch

    def rows(n):
        return pl.BlockSpec((None, n, LANES), lambda b: (b, 0, 0))

    return pl.pallas_call(
        functools.partial(_decay_kernel, n_heads=n_heads),
        grid=(batch,),
        in_specs=[rows(seq), rows(N_META), _const_spec(sel.shape)],
        out_specs=[rows(seq), rows(seq), rows(N_META), rows(N_META)],
        out_shape=[jax.ShapeDtypeStruct((batch, seq, LANES), BF16)] * 2
        + [jax.ShapeDtypeStruct((batch, N_META, LANES), BF16)] * 2,
        scratch_shapes=[pltpu.VMEM((LANES + seq, LANES), F32)],
        compiler_params=_params(1),
        name="decay",
    )(logf_r.reshape(batch, seq, LANES), logf_m.reshape(batch, N_META, LANES), sel)


def _fox_kernel(q_ref, k_ref, v_ref, qm_ref, km_ref, vm_ref, qaug_ref, kaug_ref, qaugm_ref, kaugm_ref,
                o_ref, om_ref, k_scr, km_scr, vt_scr, vmt_scr, vm_scr, s_scr, sm_scr, p_scr, pm_scr, qt_scr,
                *, tile, score_rows):
    seq = q_ref.shape[0]
    n_qblk = seq // tile
    pad_rows = LANES - N_META
    pair = pl.program_id(1)
    lane = lax.broadcasted_iota(jnp.int32, (1, LANES), 1)
    own = [lane < FOX_HEAD_DIM, lane >= FOX_HEAD_DIM]
    free0 = [FOX_HEAD_DIM, 0]
    n_slots = s_scr.shape[0]

    def augment(a, x, aug):
        lane0 = free0[a] + BIAS_LANES * pair
        mine = (lane >= lane0) & (lane < lane0 + BIAS_LANES)
        return jnp.where(own[a], x, jnp.where(mine, aug, jnp.zeros_like(aug)))

    pad = jnp.zeros((pad_rows, LANES), BF16)
    for a in (0, 1):
        ones_col = jnp.where(lane == free0[a], 1.0, 0.0).astype(BF16)
        km_scr[a] = jnp.concatenate([augment(a, km_ref[...], kaugm_ref[...]), pad], axis=0)
        v_meta = jnp.concatenate([jnp.where(own[a], vm_ref[...], ones_col), pad], axis=0)
        vm_scr[a] = v_meta
        vmt_scr[a] = v_meta.T
        for r0 in range(0, seq, tile):
            rows = slice(r0, r0 + tile)
            k_scr[a, rows, :] = augment(a, k_ref[rows, :], kaug_ref[rows, :])
            vt_scr[a, :, rows] = jnp.where(own[a], v_ref[rows, :], ones_col).T
    for slot in range(n_slots):
        pm_scr[slot, N_META:LANES, :] = jnp.zeros((pad_rows, tile), BF16)

    def meta_queries():
        row_m = lax.broadcasted_iota(jnp.int32, (N_META, LANES), 0)
        col_m = lax.broadcasted_iota(jnp.int32, (N_META, LANES), 1)
        outs = []
        for a in (0, 1):
            s = _dot_nt(augment(a, qm_ref[...], qaugm_ref[...]), km_scr[a])
            s = jnp.where(col_m <= row_m, s, -jnp.inf)
            p = jnp.exp2(s - jnp.max(s, axis=1, keepdims=True)).astype(BF16)
            acc = _dot(p, vm_scr[a])
            outs.append(acc * (1.0 / acc[:, free0[a]:free0[a] + 1]))
        om_ref[...] = jnp.where(own[0], outs[0], outs[1]).astype(om_ref.dtype)

    def score_stage(i, a):
        slot = (2 * i + a) % n_slots
        d0, kt = i * tile, (i + 1) * tile
        qt_scr[slot] = augment(a, q_ref[d0:kt, :], qaug_ref[d0:kt, :]).T
        qt = qt_scr[slot]
        sm = _dot(km_scr[a, 0:N_META, :], qt)
        sm_scr[slot] = sm
        mx = jnp.max(sm.reshape(N_META // 8, 8, tile), axis=0)
        for r0 in range(0, kt, score_rows):
            rows = min(score_rows, kt - r0)
            s = _dot(k_scr[a, r0:r0 + rows, :], qt)
            if r0 + rows > d0:
                key_i = lax.broadcasted_iota(jnp.int32, (rows, tile), 0) + (r0 - d0)
                qry_i = lax.broadcasted_iota(jnp.int32, (rows, tile), 1)
                s = jnp.where(key_i <= qry_i, s, -jnp.inf)
            s_scr[slot, r0:r0 + rows, :] = s
            mx = jnp.maximum(mx, jnp.max(s.reshape(rows // 8, 8, tile), axis=0))
        return jnp.max(mx, axis=0, keepdims=True)

    def value_stage(i, maxes_i):
        kt = (i + 1) * tile
        slots = [(2 * i + a) % n_slots for a in (0, 1)]
        for a in (0, 1):
            pm_scr[slots[a], 0:N_META, :] = jnp.exp2(sm_scr[slots[a]] - maxes_i[a]).astype(BF16)
        for r0 in range(0, kt, LANES):
            for a in (0, 1):
                p_scr[slots[a], r0:r0 + LANES, :] = jnp.exp2(
                    s_scr[slots[a], r0:r0 + LANES, :] - maxes_i[a]).astype(BF16)
        outs = []
        for a in (0, 1):
            acc_t = (_dot(vt_scr[a, :, 0:kt], p_scr[slots[a], 0:kt, :])
                     + _dot(vmt_scr[a], pm_scr[slots[a]]))
            l = acc_t[free0[a]:free0[a] + 1, :]
            outs.append((acc_t * (1.0 / l)).T)
        return outs

    ahead = n_slots // 2 - 1
    maxes = {j: [score_stage(j, a) for a in (0, 1)] for j in range(min(ahead, n_qblk))}
    for i in range(n_qblk):
        if i + ahead < n_qblk:
            maxes[i + ahead] = [score_stage(i + ahead, a) for a in (0, 1)]
        outs = value_stage(i, maxes[i])
        o_ref[i * tile:(i + 1) * tile, :] = jnp.where(own[0], outs[0], outs[1]).astype(o_ref.dtype)
        if i == 0:
            meta_queries()


def _fox(fox_r, fox_m, qaug, kaug, qaug_m, kaug_m, *, batch):
    n_r, cols = fox_r.shape
    seq = n_r // batch
    width = cols // 3
    n_pairs = width // LANES
    tile = FOX_TILE
    assert seq % tile == 0 and BIAS_LANES * n_pairs <= FOX_HEAD_DIM
    qkv_r = fox_r.reshape(batch, seq, cols)
    qkv_m = fox_m.reshape(batch, N_META, cols)
    kern = functools.partial(_fox_kernel, tile=tile, score_rows=FOX_SCORE_ROWS)

    def cols_of(rows, j0):
        return pl.BlockSpec((None, rows, LANES), lambda b, p: (b, 0, j0 + p))

    def shared(rows):
        return pl.BlockSpec((None, rows, LANES), lambda b, p: (b, 0, 0))

    return pl.pallas_call(
        kern,
        grid=(batch, n_pairs),
        in_specs=[cols_of(seq, 0), cols_of(seq, n_pairs), cols_of(seq, 2 * n_pairs),
                  cols_of(N_META, 0), cols_of(N_META, n_pairs), cols_of(N_META, 2 * n_pairs),
                  shared(seq), shared(seq), shared(N_META), shared(N_META)],
        out_specs=[cols_of(seq, 0), cols_of(N_META, 0)],
        out_shape=[jax.ShapeDtypeStruct((batch, seq, width), BF16),
                   jax.ShapeDtypeStruct((batch, N_META, width), BF16)],
        scratch_shapes=[
            pltpu.VMEM((2, seq, LANES), BF16),
            pltpu.VMEM((2, LANES, LANES), BF16),
            pltpu.VMEM((2, LANES, seq), BF16),
            pltpu.VMEM((2, LANES, LANES), BF16),
            pltpu.VMEM((2, LANES, LANES), BF16),
            pltpu.VMEM((FOX_SLOTS, seq, tile), F32),
            pltpu.VMEM((FOX_SLOTS, N_META, tile), F32),
            pltpu.VMEM((FOX_SLOTS, seq, tile), BF16),
            pltpu.VMEM((FOX_SLOTS, LANES, tile), BF16),
            pltpu.VMEM((FOX_SLOTS, LANES, tile), BF16),
        ],
        compiler_params=_params(2),
        name="fox",
    )(qkv_r, qkv_r, qkv_r, qkv_m, qkv_m, qkv_m, qaug, kaug, qaug_m, kaug_m)


def _ret_kernel(q_ref, kt_ref, v_ref, g_ref, qm_ref, km_ref, vm_ref, gm_ref, gn_ref,
                o_ref, om_ref, state_scr, dec_scr):
    C = RET_CHUNK
    n_chunks = q_ref.shape[0] // C
    ri = lax.broadcasted_iota(jnp.int32, (C, C), 0)
    ci = lax.broadcasted_iota(jnp.int32, (C, C), 1)
    diff = ri - ci
    row_f = ri.astype(F32)

    for hh in range(RET_HEADS):
        log_g = RET_LOG_GAMMA[hh]
        dec_scr[hh, 0] = jnp.where(diff >= 0, jnp.exp(jnp.maximum(diff, 0).astype(F32) * log_g), 0.0)
        dec_scr[hh, 1] = jnp.exp((row_f + 1.0) * log_g)
        dec_scr[hh, 2] = jnp.exp((C - 1.0 - ci.astype(F32)) * log_g)
        dec_scr[hh, 3] = jnp.exp((N_META - 1.0 - row_f) * log_g)

    def head(hh):
        return slice(hh * RET_HEAD_DIM, (hh + 1) * RET_HEAD_DIM)

    def finish(o, g, out_ref, rows, hh):
        mu = jnp.mean(o, axis=-1, keepdims=True)
        d = o - mu
        var = jnp.mean(d * d, axis=-1, keepdims=True)
        y = d * lax.rsqrt(var + EPS) * gn_ref[:, head(hh)]
        g = g.astype(F32)
        out_ref[rows, head(hh)] = (g * (1.0 / (1.0 + jnp.exp(-g))) * y).astype(out_ref.dtype)

    def intra(q, k, v, hh):
        s = _dot_nt(q, k) * dec_scr[hh, 0]
        return _dot(s.astype(BF16), v)

    def decayed_keys(k, hh, table):
        return (k.astype(F32) * dec_scr[hh, table]).astype(BF16)

    pad = jnp.zeros((C - N_META, RET_HEAD_DIM), BF16)
    for hh in range(RET_HEADS):
        q, k, v = (jnp.concatenate([r[:, head(hh)], pad], axis=0) for r in (qm_ref, km_ref, vm_ref))
        finish(intra(q, k, v, hh)[0:N_META, :], gm_ref[:, head(hh)], om_ref, slice(0, N_META), hh)
        state_scr[hh] = _dot_tn(decayed_keys(k, hh, 3), v)

    def rows_of(c):
        return slice(c * C, (c + 1) * C)

    def scores(c):
        out = []
        for hh in range(RET_HEADS):
            q, v = q_ref[rows_of(c), head(hh)], v_ref[rows_of(c), head(hh)]
            k_t = kt_ref[head(hh), rows_of(c)]
            out.append((_dot(q, k_t), _dot(decayed_keys(k_t, hh, 2), v)))
        return out

    def carry(c, kv):
        out = []
        for hh in range(RET_HEADS):
            state = state_scr[hh]
            out.append(_dot(q_ref[rows_of(c), head(hh)], state.astype(BF16)) * dec_scr[hh, 1])
            chunk_dec = jnp.exp(jnp.full((1, 1), C * RET_LOG_GAMMA[hh], F32))
            state_scr[hh] = chunk_dec * state + kv[hh]
        return out

    def outputs(c, s, inter):
        for hh in range(RET_HEADS):
            o = _dot((s[hh] * dec_scr[hh, 0]).astype(BF16), v_ref[rows_of(c), head(hh)]) + inter[hh]
            finish(o, g_ref[rows_of(c), head(hh)], o_ref, rows_of(c), hh)

    cur = scores(0)
    inter = carry(0, [kv for _, kv in cur])
    for c in range(n_chunks):
        nxt = scores(c + 1) if c + 1 < n_chunks else None
        outputs(c, [s for s, _ in cur], inter)
        if nxt is not None:
            inter = carry(c + 1, [kv for _, kv in nxt])
        cur = nxt


def _retention(ret_r, kt_r, ret_m, gn, *, batch):
    n_r, cols = ret_r.shape
    seq = n_r // batch
    width = cols // 3
    x_r = ret_r.reshape(batch, seq, cols)
    x_m = ret_m.reshape(batch, N_META, 4 * width)

    def part(rows, j):
        return pl.BlockSpec((None, rows, width), lambda b: (b, 0, j))

    return pl.pallas_call(
        _ret_kernel,
        grid=(batch,),
        in_specs=[part(seq, 0), pl.BlockSpec((None, width, seq), lambda b: (b, 0, 0)), part(seq, 1), part(seq, 2)]
        + [part(N_META, j) for j in range(4)] + [_const_spec(gn.shape)],
        out_specs=[part(seq, 0), part(N_META, 0)],
        out_shape=[jax.ShapeDtypeStruct((batch, seq, width), BF16),
                   jax.ShapeDtypeStruct((batch, N_META, width), BF16)],
        scratch_shapes=[pltpu.VMEM((RET_HEADS, RET_HEAD_DIM, RET_HEAD_DIM), F32),
                        pltpu.VMEM((RET_HEADS, 4, RET_CHUNK, RET_HEAD_DIM), F32)],
        compiler_params=_params(1),
        name="retention",
    )(x_r, kt_r, x_r, x_r, x_m, x_m, x_m, x_m, gn)


def _outffn_body(h_ref, fox_ref, ret_ref, wo_ref, g2_ref, wg_ref, wu_ref, wd_ref, gf_ref, o_ref,
                 hn_scr, act_scr, *, final):
    rows = h_ref.shape[0]
    fox_w = fox_ref.shape[1]
    halves = [slice(0, rows // 2), slice(rows // 2, rows)]
    for r in halves:
        h1 = h_ref[r, :] + _dot(fox_ref[r, :], wo_ref[0:fox_w, :]) + _dot(ret_ref[r, :], wo_ref[fox_w:, :])
        hn_scr[r, :] = _rmsnorm(h1, g2_ref[...]).astype(BF16)
        o_ref[r, :] = h1
    for c0 in range(0, wg_ref.shape[1], FF_CHUNK):
        for r in halves:
            hn = hn_scr[r, :]
            gate = _dot(hn, wg_ref[:, c0:c0 + FF_CHUNK])
            up = _dot(hn, wu_ref[:, c0:c0 + FF_CHUNK])
            act_scr[r, c0:c0 + FF_CHUNK] = (gate * (1.0 / (1.0 + jnp.exp(-gate))) * up).astype(BF16)
    for r in halves:
        out = o_ref[r, :] + _dot(act_scr[r, :], wd_ref[...])
        if final:
            out = _rmsnorm(out, gf_ref[...])
        o_ref[r, :] = out


def _outffn_kernel(*refs, n_tiles, final):
    if final:
        (hr, foxr, retr, wo, g2, wg, wu, wd, gf, outr, hn_scr, act_scr) = refs
    else:
        (hr, foxr, retr, hm, foxm, retm, wo, g2, wg, wu, wd, gf, outr, outm, hn_scr, act_scr) = refs
    body = functools.partial(_outffn_body, final=final)
    t = pl.program_id(0)

    @pl.when(t < n_tiles)
    def _():
        body(hr, foxr, retr, wo, g2, wg, wu, wd, gf, outr, hn_scr, act_scr)

    if not final:
        @pl.when(t == n_tiles)
        def _():
            body(hm, foxm, retm, wo, g2, wg, wu, wd, gf, outm, hn_scr, act_scr)


def _outffn(h_r, fox_r, ret_r, h_m, fox_m, ret_m, wo, g2, wg, wu, wd, gf, *, layer, final):
    n_r, d = h_r.shape
    n_m = h_m.shape[0]
    d_ff = wg.shape[2]
    tm = _row_tile(n_r, OUTFFN_ROWS)
    n_tiles = n_r // tm
    assert n_m <= tm

    def real(cols):
        return pl.BlockSpec((tm, cols), lambda t: (jnp.minimum(t, n_tiles - 1), 0))

    def meta(cols):
        return pl.BlockSpec((n_m, cols), lambda t: (0, 0))

    weights = [_layer_spec(wo.shape, layer), _const_spec(g2.shape), _layer_spec(wg.shape, layer),
               _layer_spec(wu.shape, layer), _layer_spec(wd.shape, layer), _const_spec(gf.shape)]
    real_in = [real(d), real(fox_r.shape[1]), real(ret_r.shape[1])]
    kern = functools.partial(_outffn_kernel, n_tiles=n_tiles, final=final)
    scratch = [pltpu.VMEM((tm, d), BF16),
               pltpu.VMEM((tm, d_ff), BF16)]
    if final:
        return pl.pallas_call(
            kern, grid=(n_tiles,), in_specs=real_in + weights, out_specs=real(d),
            out_shape=jax.ShapeDtypeStruct((n_r, d), F32), scratch_shapes=scratch,
            compiler_params=_params(1), name="outffn_last",
        )(h_r, fox_r, ret_r, wo, g2, wg, wu, wd, gf), None
    meta_in = [meta(d), meta(fox_m.shape[1]), meta(ret_m.shape[1])]
    return pl.pallas_call(
        kern, grid=(n_tiles + 1,), in_specs=real_in + meta_in + weights,
        out_specs=[real(d), meta(d)],
        out_shape=[jax.ShapeDtypeStruct((n_r, d), F32), jax.ShapeDtypeStruct((n_m, d), F32)],
        scratch_shapes=scratch, compiler_params=_params(1), name="outffn",
    )(h_r, fox_r, ret_r, h_m, fox_m, ret_m, wo, g2, wg, wu, wd, gf)


def kernel(x, meta_tokens, attn_norm, w_in, b_fgate, ret_gn, w_out, ffn_norm,
           w_gate, w_up, w_down, final_norm):
    batch, seq, d = x.shape
    depth = w_in.shape[0]
    fox_heads = b_fgate.shape[1]
    fox_w = fox_heads * FOX_HEAD_DIM
    ret_w = RET_HEADS * RET_HEAD_DIM
    d_ff = w_gate.shape[2]
    assert w_in.shape[2] == 3 * fox_w + fox_heads + 4 * ret_w
    assert fox_w % LANES == 0 and fox_heads <= LANES and d_ff % FF_CHUNK == 0
    assert seq % FOX_TILE == 0 and seq % RET_CHUNK == 0

    h_r = x.reshape(batch * seq, d)
    h_m = jnp.broadcast_to(meta_tokens[None].astype(x.dtype), (batch, N_META, d)).reshape(batch * N_META, d)

    inv_freq = ROPE_BASE ** (-jnp.arange(0, RET_HEAD_DIM, 2, dtype=F32) / RET_HEAD_DIM)
    ang = jnp.arange(seq + N_META, dtype=F32)[:, None] * inv_freq[None, :]
    cos, sin = jnp.cos(ang), jnp.sin(ang)
    cs = jnp.concatenate([cos, cos], axis=1)
    sn = jnp.concatenate([-sin, sin], axis=1)
    cs_r, sn_r = cs[N_META:], sn[N_META:]
    cs_m, sn_m = jnp.tile(cs[:N_META], (batch, 1)), jnp.tile(sn[:N_META], (batch, 1))

    split = 3 * fox_w
    w_all = jnp.concatenate(
        [w_in[..., :split], w_in[..., split + fox_heads:], w_in[..., split:split + fox_heads],
         jnp.zeros((depth, d, LANES - fox_heads), w_in.dtype)], axis=-1).astype(BF16)
    wo, wg, wu, wd = (w.astype(BF16) for w in (w_out, w_gate, w_up, w_down))
    sel = jnp.asarray(_bias_selector(fox_heads), BF16)
    gf = final_norm.reshape(1, d)

    for i in range(depth):
        bl = jnp.pad(b_fgate[i], (0, LANES - fox_heads)).reshape(1, LANES)
        fox_r, ret_r, logf_r, kt_r, fox_m, ret_m, logf_m = _inproj(
            h_r, h_m, cs_r, sn_r, cs_m, sn_m, attn_norm[i].reshape(1, d), w_all, bl,
            layer=i, seq=seq, fox_w=fox_w, ret_w=ret_w)
        qaug, kaug, qaug_m, kaug_m = _decay(logf_r, logf_m, sel, batch=batch, n_heads=fox_heads)
        fo_r, fo_m = _fox(fox_r, fox_m, qaug, kaug, qaug_m, kaug_m, batch=batch)
        ro_r, ro_m = _retention(ret_r, kt_r, ret_m, ret_gn[i].reshape(1, ret_w), batch=batch)
        h_r, h_m = _outffn(h_r, fo_r.reshape(batch * seq, fox_w), ro_r.reshape(batch * seq, ret_w),
                           h_m, fo_m.reshape(batch * N_META, fox_w), ro_m.reshape(batch * N_META, ret_w),
                           wo, ffn_norm[i].reshape(1, d), wg, wu, wd, gf,
                           layer=i, final=(i == depth - 1))
    return h_r.reshape(batch, seq, d)
```

```python
import functools
import math

import numpy as np
import jax
import jax.numpy as jnp
from jax import lax
from jax.experimental import pallas as pl
from jax.experimental.pallas import tpu as pltpu

N_META = 16
FOX_HEAD_DIM = 64
RET_HEADS = 4
RET_HEAD_DIM = 128
RET_CHUNK = 128
ROPE_BASE = 10000.0
EPS = 1e-6
RET_LOG_GAMMA = tuple(math.log(1.0 - 2.0 ** (-5 - h)) for h in range(RET_HEADS))
LOG2E = math.log2(math.e)

LANES = 128
BF16_SUBLANES = 16
VMEM_LIMIT_BYTES = 56 * 1024 * 1024

N_SPLIT = 3
BIAS_LANES = 2 * N_SPLIT

INPROJ_ROWS = 1024
OUTFFN_ROWS = 1024
FF_CHUNK = 256
FOX_TILE = 256
FOX_SCORE_ROWS = 128
FOX_SLOTS = 6

F32 = jnp.float32
BF16 = jnp.bfloat16

assert RET_CHUNK == RET_HEAD_DIM == LANES


def _dot(a, b):
    return jnp.dot(a, b, preferred_element_type=F32)


def _dot_nt(a, b):
    return lax.dot_general(a, b, (((1,), (1,)), ((), ())), preferred_element_type=F32)


def _dot_tn(a, b):
    return lax.dot_general(a, b, (((0,), (0,)), ((), ())), preferred_element_type=F32)


def _rmsnorm(x, g):
    return x * lax.rsqrt(jnp.mean(x * x, axis=-1, keepdims=True) + EPS) * g


def _row_tile(n, cap):
    best = None
    for t in range(BF16_SUBLANES, min(n, cap) + 1, BF16_SUBLANES):
        if n % t == 0:
            best = t
    assert best is not None, n
    return best


def _const_spec(shape):
    nd = len(shape)
    return pl.BlockSpec(shape, lambda *_: (0,) * nd, pipeline_mode=pl.Buffered(1))


def _layer_spec(shape, layer):
    nd = len(shape) - 1
    return pl.BlockSpec((None,) + tuple(shape[1:]), lambda *_: (layer,) + (0,) * nd,
                        pipeline_mode=pl.Buffered(1))


def _params(n_axes):
    return pltpu.CompilerParams(dimension_semantics=("arbitrary",) * n_axes,
                                vmem_limit_bytes=VMEM_LIMIT_BYTES)


def _inproj_body(h_ref, cs_ref, sn_ref, g_ref, w_ref, bl_ref, fox_ref, ret_ref, logf_ref, kt_ref=None,
                 *, fox_w, ret_w):
    ret_col = {0: 0, 1: ret_w, 2: 2 * ret_w, 3: 3 * ret_w} if kt_ref is None else {0: 0, 2: ret_w, 3: 2 * ret_w}
    rows = h_ref.shape[0]
    halves = [slice(0, rows // 2), slice(rows // 2, rows)]
    xns = [_rmsnorm(h_ref[r, :], g_ref[...]).astype(BF16) for r in halves]
    q_scale = FOX_HEAD_DIM ** -0.5 * LOG2E
    k_scale = RET_HEAD_DIM ** -0.5
    base = 3 * fox_w
    for part in range(4):
        for r, xn in zip(halves, xns):
            y = _dot(xn, w_ref[:, base + part * ret_w:base + (part + 1) * ret_w])
            if part < 2:
                cs = cs_ref[r, :]
                sn = sn_ref[r, :]
                for hh in range(RET_HEADS):
                    sl = slice(hh * RET_HEAD_DIM, (hh + 1) * RET_HEAD_DIM)
                    yh = y[:, sl]
                    yh = yh * cs + pltpu.roll(yh, RET_HEAD_DIM // 2, axis=1) * sn
                    if part == 1:
                        yh = yh * k_scale
                    if part == 1 and kt_ref is not None:
                        kt_ref[sl, r] = yh.astype(BF16).T
                    else:
                        ret_ref[r, ret_col[part] + sl.start:ret_col[part] + sl.stop] = yh.astype(BF16)
            else:
                ret_ref[r, ret_col[part]:ret_col[part] + ret_w] = y.astype(BF16)
    for r, xn in zip(halves, xns):
        z = _dot(xn, w_ref[:, base + 4 * ret_w:base + 4 * ret_w + LANES]) + bl_ref[...]
        logf_ref[r, :] = jnp.minimum(z, 0.0) - jnp.log(1.0 + jnp.exp(-jnp.abs(z)))
    for c0 in range(0, 3 * fox_w, fox_w):
        for r, xn in zip(halves, xns):
            y = _dot(xn, w_ref[:, c0:c0 + fox_w])
            if c0 == 0:
                y = y * q_scale
            fox_ref[r, c0:c0 + fox_w] = y.astype(BF16)


def _inproj_kernel(hr_ref, hm_ref, csr_ref, snr_ref, csm_ref, snm_ref, g_ref, w_ref, bl_ref,
                   foxr_ref, retr_ref, logfr_ref, ktr_ref, foxm_ref, retm_ref, logfm_ref,
                   *, n_tiles, fox_w, ret_w):
    body = functools.partial(_inproj_body, fox_w=fox_w, ret_w=ret_w)
    t = pl.program_id(0)

    @pl.when(t < n_tiles)
    def _():
        body(hr_ref, csr_ref, snr_ref, g_ref, w_ref, bl_ref, foxr_ref, retr_ref, logfr_ref, ktr_ref)

    @pl.when(t == n_tiles)
    def _():
        body(hm_ref, csm_ref, snm_ref, g_ref, w_ref, bl_ref, foxm_ref, retm_ref, logfm_ref)


def _inproj(h_r, h_m, cs_r, sn_r, cs_m, sn_m, g, w_all, bl, *, layer, seq, fox_w, ret_w):
    n_r, d = h_r.shape
    n_m = h_m.shape[0]
    tm = _row_tile(seq, INPROJ_ROWS)
    n_tiles = n_r // tm
    per_seq = seq // tm
    fox_cols, ret_cols = 3 * fox_w, 4 * ret_w

    def real(cols):
        return pl.BlockSpec((tm, cols), lambda t: (jnp.minimum(t, n_tiles - 1), 0))

    def table():
        return pl.BlockSpec((tm, LANES), lambda t: (jnp.minimum(t, n_tiles - 1) % per_seq, 0))

    def meta(cols):
        return pl.BlockSpec((n_m, cols), lambda t: (0, 0))

    def transposed_keys():
        return pl.BlockSpec((None, ret_w, tm), lambda t: (jnp.minimum(t, n_tiles - 1) // per_seq, 0,
                                                           jnp.minimum(t, n_tiles - 1) % per_seq))

    kern = functools.partial(_inproj_kernel, n_tiles=n_tiles, fox_w=fox_w, ret_w=ret_w)
    return pl.pallas_call(
        kern,
        grid=(n_tiles + 1,),
        in_specs=[real(d), meta(d), table(), table(), meta(LANES), meta(LANES),
                  _const_spec((1, d)), _layer_spec(w_all.shape, layer), _const_spec((1, LANES))],
        out_specs=[real(fox_cols), real(ret_cols - ret_w), real(LANES), transposed_keys(),
                   meta(fox_cols), meta(ret_cols), meta(LANES)],
        out_shape=[
            jax.ShapeDtypeStruct((n_r, fox_cols), BF16),
            jax.ShapeDtypeStruct((n_r, ret_cols - ret_w), BF16),
            jax.ShapeDtypeStruct((n_r, LANES), F32),
            jax.ShapeDtypeStruct((n_r // seq, ret_w, seq), BF16),
            jax.ShapeDtypeStruct((n_m, fox_cols), BF16),
            jax.ShapeDtypeStruct((n_m, ret_cols), BF16),
            jax.ShapeDtypeStruct((n_m, LANES), F32),
        ],
        compiler_params=_params(1),
        name="inproj",
    )(h_r, h_m, cs_r, sn_r, cs_m, sn_m, g, w_all, bl)


def _bias_lane(pair, a):
    return (FOX_HEAD_DIM if a == 0 else 0) + BIAS_LANES * pair


def _split3(x):
    hi = x.astype(BF16)
    r1 = x - hi.astype(F32)
    mid = r1.astype(BF16)
    lo = (r1 - mid.astype(F32)).astype(BF16)
    return hi, mid, lo


def _bias_selector(n_heads):
    sel = np.zeros((N_SPLIT * LANES, LANES), np.float32)
    for head in range(n_heads):
        lane0 = _bias_lane(head // 2, head % 2)
        for p in range(N_SPLIT):
            sel[p * LANES + head, lane0 + p] = 1.0
            sel[p * LANES + head, lane0 + N_SPLIT + p] = -1.0
    return sel


def _decay_kernel(logf_ref, logfm_ref, sel_ref, qaug_ref, kaug_ref, qaugm_ref, kaugm_ref, c_scr,
                  *, n_heads):
    blk = LANES
    seq = logf_ref.shape[0]
    ri = lax.broadcasted_iota(jnp.int32, (blk, blk), 0)
    ci = lax.broadcasted_iota(jnp.int32, (blk, blk), 1)
    tri = (ci <= ri).astype(BF16)

    def prefix(x):
        hi, mid, lo = _split3(x)
        return _dot(tri, hi) + _dot(tri, mid) + _dot(tri, lo)

    cb = prefix(jnp.concatenate([logfm_ref[...], jnp.zeros((blk - N_META, LANES), F32)], axis=0))
    c_scr[0:blk, :] = cb
    carry = cb[N_META - 1:N_META, :]
    for r0 in range(0, seq, blk):
        cb = prefix(logf_ref[r0:r0 + blk, :]) + carry
        c_scr[blk + r0:2 * blk + r0, :] = cb
        carry = cb[blk - 1:blk, :]

    in_half = lax.broadcasted_iota(jnp.int32, (1, LANES), 1) % FOX_HEAD_DIM
    used = in_half < BIAS_LANES * (n_heads // 2)
    first = used & (in_half % BIAS_LANES < N_SPLIT)
    second = used & (in_half % BIAS_LANES >= N_SPLIT)

    def route(c, q_out, k_out):
        hi, mid, lo = _split3(c * LOG2E)
        routed = _dot(jnp.concatenate([hi, mid, lo], axis=1), sel_ref[...])
        q_out[...] = jnp.where(first, routed, jnp.where(second, 1.0, 0.0)).astype(BF16)
        k_out[...] = jnp.where(second, routed, jnp.where(first, 1.0, 0.0)).astype(BF16)

    route(c_scr[0:N_META, :], qaugm_ref, kaugm_ref)
    route(c_scr[blk:, :], qaug_ref, kaug_ref)


def _decay(logf_r, logf_m, sel, *, batch, n_heads):
    seq = logf_r.shape[0] // batch

    def rows(n):
        return pl.BlockSpec((None, n, LANES), lambda b: (b, 0, 0))

    return pl.pallas_call(
        functools.partial(_decay_kernel, n_heads=n_heads),
        grid=(batch,),
        in_specs=[rows(seq), rows(N_META), _const_spec(sel.shape)],
        out_specs=[rows(seq), rows(seq), rows(N_META), rows(N_META)],
        out_shape=[jax.ShapeDtypeStruct((batch, seq, LANES), BF16)] * 2
        + [jax.ShapeDtypeStruct((batch, N_META, LANES), BF16)] * 2,
        scratch_shapes=[pltpu.VMEM((LANES + seq, LANES), F32)],
        compiler_params=_params(1),
        name="decay",
    )(logf_r.reshape(batch, seq, LANES), logf_m.reshape(batch, N_META, LANES), sel)


def _fox_kernel(q_ref, k_ref, v_ref, qm_ref, km_ref, vm_ref, qaug_ref, kaug_ref, qaugm_ref, kaugm_ref,
                o_ref, om_ref, k_scr, km_scr, vt_scr, vmt_scr, vm_scr, s_scr, sm_scr, p_scr, pm_scr, qt_scr,
                *, tile, score_rows):
    seq = q_ref.shape[0]
    n_qblk = seq // tile
    pad_rows = LANES - N_META
    pair = pl.program_id(1)
    lane = lax.broadcasted_iota(jnp.int32, (1, LANES), 1)
    own = [lane < FOX_HEAD_DIM, lane >= FOX_HEAD_DIM]
    free0 = [FOX_HEAD_DIM, 0]
    n_slots = s_scr.shape[0]

    def augment(a, x, aug):
        lane0 = free0[a] + BIAS_LANES * pair
        mine = (lane >= lane0) & (lane < lane0 + BIAS_LANES)
        return jnp.where(own[a], x, jnp.where(mine, aug, jnp.zeros_like(aug)))

    pad = jnp.zeros((pad_rows, LANES), BF16)
    for a in (0, 1):
        ones_col = jnp.where(lane == free0[a], 1.0, 0.0).astype(BF16)
        km_scr[a] = jnp.concatenate([augment(a, km_ref[...], kaugm_ref[...]), pad], axis=0)
        v_meta = jnp.concatenate([jnp.where(own[a], vm_ref[...], ones_col), pad], axis=0)
        vm_scr[a] = v_meta
        vmt_scr[a] = v_meta.T
        for r0 in range(0, seq, tile):
            rows = slice(r0, r0 + tile)
            k_scr[a, rows, :] = augment(a, k_ref[rows, :], kaug_ref[rows, :])
            vt_scr[a, :, rows] = jnp.where(own[a], v_ref[rows, :], ones_col).T
    for slot in range(n_slots):
        pm_scr[slot, N_META:LANES, :] = jnp.zeros((pad_rows, tile), BF16)

    def meta_queries():
        row_m = lax.broadcasted_iota(jnp.int32, (N_META, LANES), 0)
        col_m = lax.broadcasted_iota(jnp.int32, (N_META, LANES), 1)
        outs = []
        for a in (0, 1):
            s = _dot_nt(augment(a, qm_ref[...], qaugm_ref[...]), km_scr[a])
            s = jnp.where(col_m <= row_m, s, -jnp.inf)
            p = jnp.exp2(s - jnp.max(s, axis=1, keepdims=True)).astype(BF16)
            acc = _dot(p, vm_scr[a])
            outs.append(acc * (1.0 / acc[:, free0[a]:free0[a] + 1]))
        om_ref[...] = jnp.where(own[0], outs[0], outs[1]).astype(om_ref.dtype)

    def score_stage(i, a):
        slot = (2 * i + a) % n_slots
        d0, kt = i * tile, (i + 1) * tile
        qt_scr[slot] = augment(a, q_ref[d0:kt, :], qaug_ref[d0:kt, :]).T
        qt = qt_scr[slot]
        sm = _dot(km_scr[a, 0:N_META, :], qt)
        sm_scr[slot] = sm
        mx = jnp.max(sm.reshape(N_META // 8, 8, tile), axis=0)
        for r0 in range(0, kt, score_rows):
            rows = min(score_rows, kt - r0)
            s = _dot(k_scr[a, r0:r0 + rows, :], qt)
            if r0 + rows > d0:
                key_i = lax.broadcasted_iota(jnp.int32, (rows, tile), 0) + (r0 - d0)
                qry_i = lax.broadcasted_iota(jnp.int32, (rows, tile), 1)
                s = jnp.where(key_i <= qry_i, s, -jnp.inf)
            s_scr[slot, r0:r0 + rows, :] = s
            mx = jnp.maximum(mx, jnp.max(s.reshape(rows // 8, 8, tile), axis=0))
        return jnp.max(mx, axis=0, keepdims=True)

    def value_stage(i, maxes_i):
        kt = (i + 1) * tile
        slots = [(2 * i + a) % n_slots for a in (0, 1)]
        for a in (0, 1):
            pm_scr[slots[a], 0:N_META, :] = jnp.exp2(sm_scr[slots[a]] - maxes_i[a]).astype(BF16)
        for r0 in range(0, kt, LANES):
            for a in (0, 1):
                p_scr[slots[a], r0:r0 + LANES, :] = jnp.exp2(
                    s_scr[slots[a], r0:r0 + LANES, :] - maxes_i[a]).astype(BF16)
        outs = []
        for a in (0, 1):
            acc_t = (_dot(vt_scr[a, :, 0:kt], p_scr[slots[a], 0:kt, :])
                     + _dot(vmt_scr[a], pm_scr[slots[a]]))
            l = acc_t[free0[a]:free0[a] + 1, :]
            outs.append((acc_t * (1.0 / l)).T)
        return outs

    ahead = n_slots // 2 - 1
    maxes = {j: [score_stage(j, a) for a in (0, 1)] for j in range(min(ahead, n_qblk))}
    for i in range(n_qblk):
        if i + ahead < n_qblk:
            maxes[i + ahead] = [score_stage(i + ahead, a) for a in (0, 1)]
        outs = value_stage(i, maxes[i])
        o_ref[i * tile:(i + 1) * tile, :] = jnp.where(own[0], outs[0], outs[1]).astype(o_ref.dtype)
        if i == 0:
            meta_queries()


def _fox(fox_r, fox_m, qaug, kaug, qaug_m, kaug_m, *, batch):
    n_r, cols = fox_r.shape
    seq = n_r // batch
    width = cols // 3
    n_pairs = width // LANES
    tile = FOX_TILE
    assert seq % tile == 0 and BIAS_LANES * n_pairs <= FOX_HEAD_DIM
    qkv_r = fox_r.reshape(batch, seq, cols)
    qkv_m = fox_m.reshape(batch, N_META, cols)
    kern = functools.partial(_fox_kernel, tile=tile, score_rows=FOX_SCORE_ROWS)

    def cols_of(rows, j0):
        return pl.BlockSpec((None, rows, LANES), lambda b, p: (b, 0, j0 + p))

    def shared(rows):
        return pl.BlockSpec((None, rows, LANES), lambda b, p: (b, 0, 0))

    return pl.pallas_call(
        kern,
        grid=(batch, n_pairs),
        in_specs=[cols_of(seq, 0), cols_of(seq, n_pairs), cols_of(seq, 2 * n_pairs),
                  cols_of(N_META, 0), cols_of(N_META, n_pairs), cols_of(N_META, 2 * n_pairs),
                  shared(seq), shared(seq), shared(N_META), shared(N_META)],
        out_specs=[cols_of(seq, 0), cols_of(N_META, 0)],
        out_shape=[jax.ShapeDtypeStruct((batch, seq, width), BF16),
                   jax.ShapeDtypeStruct((batch, N_META, width), BF16)],
        scratch_shapes=[
            pltpu.VMEM((2, seq, LANES), BF16),
            pltpu.VMEM((2, LANES, LANES), BF16),
            pltpu.VMEM((2, LANES, seq), BF16),
            pltpu.VMEM((2, LANES, LANES), BF16),
            pltpu.VMEM((2, LANES, LANES), BF16),
            pltpu.VMEM((FOX_SLOTS, seq, tile), F32),
            pltpu.VMEM((FOX_SLOTS, N_META, tile), F32),
            pltpu.VMEM((FOX_SLOTS, seq, tile), BF16),
            pltpu.VMEM((FOX_SLOTS, LANES, tile), BF16),
            pltpu.VMEM((FOX_SLOTS, LANES, tile), BF16),
        ],
        compiler_params=_params(2),
        name="fox",
    )(qkv_r, qkv_r, qkv_r, qkv_m, qkv_m, qkv_m, qaug, kaug, qaug_m, kaug_m)


def _ret_kernel(q_ref, kt_ref, v_ref, g_ref, qm_ref, km_ref, vm_ref, gm_ref, gn_ref,
                o_ref, om_ref, state_scr, dec_scr):
    C = RET_CHUNK
    n_chunks = q_ref.shape[0] // C
    ri = lax.broadcasted_iota(jnp.int32, (C, C), 0)
    ci = lax.broadcasted_iota(jnp.int32, (C, C), 1)
    diff = ri - ci
    row_f = ri.astype(F32)

    for hh in range(RET_HEADS):
        log_g = RET_LOG_GAMMA[hh]
        dec_scr[hh, 0] = jnp.where(diff >= 0, jnp.exp(jnp.maximum(diff, 0).astype(F32) * log_g), 0.0)
        dec_scr[hh, 1] = jnp.exp((row_f + 1.0) * log_g)
        dec_scr[hh, 2] = jnp.exp((C - 1.0 - ci.astype(F32)) * log_g)
        dec_scr[hh, 3] = jnp.exp((N_META - 1.0 - row_f) * log_g)

    def head(hh):
        return slice(hh * RET_HEAD_DIM, (hh + 1) * RET_HEAD_DIM)

    def finish(o, g, out_ref, rows, hh):
        mu = jnp.mean(o, axis=-1, keepdims=True)
        d = o - mu
        var = jnp.mean(d * d, axis=-1, keepdims=True)
        y = d * lax.rsqrt(var + EPS) * gn_ref[:, head(hh)]
        g = g.astype(F32)
        half_g = 0.5 * g
        out_ref[rows, head(hh)] = ((half_g + half_g * jnp.tanh(half_g)) * y).astype(out_ref.dtype)

    def intra(q, k, v, hh):
        s = _dot_nt(q, k) * dec_scr[hh, 0]
        return _dot(s.astype(BF16), v)

    def decayed_keys(k, hh, table):
        return (k.astype(F32) * dec_scr[hh, table]).astype(BF16)

    pad = jnp.zeros((C - N_META, RET_HEAD_DIM), BF16)
    for hh in range(RET_HEADS):
        q, k, v = (jnp.concatenate([r[:, head(hh)], pad], axis=0) for r in (qm_ref, km_ref, vm_ref))
        finish(intra(q, k, v, hh)[0:N_META, :], gm_ref[:, head(hh)], om_ref, slice(0, N_META), hh)
        state_scr[hh] = _dot_tn(decayed_keys(k, hh, 3), v)

    def rows_of(c):
        return slice(c * C, (c + 1) * C)

    def scores(c):
        out = []
        for hh in range(RET_HEADS):
            q, v = q_ref[rows_of(c), head(hh)], v_ref[rows_of(c), head(hh)]
            k_t = kt_ref[head(hh), rows_of(c)]
            out.append((_dot(q, k_t), _dot(decayed_keys(k_t, hh, 2), v)))
        return out

    def carry(c, kv):
        out = []
        for hh in range(RET_HEADS):
            state = state_scr[hh]
            out.append(_dot(q_ref[rows_of(c), head(hh)], state.astype(BF16)) * dec_scr[hh, 1])
            chunk_dec = jnp.exp(jnp.full((1, 1), C * RET_LOG_GAMMA[hh], F32))
            state_scr[hh] = chunk_dec * state + kv[hh]
        return out

    def outputs(c, s, inter):
        for hh in range(RET_HEADS):
            o = _dot((s[hh] * dec_scr[hh, 0]).astype(BF16), v_ref[rows_of(c), head(hh)]) + inter[hh]
            finish(o, g_ref[rows_of(c), head(hh)], o_ref, rows_of(c), hh)

    cur = scores(0)
    inter = carry(0, [kv for _, kv in cur])
    for c in range(n_chunks):
        nxt = scores(c + 1) if c + 1 < n_chunks else None
        outputs(c, [s for s, _ in cur], inter)
        if nxt is not None:
            inter = carry(c + 1, [kv for _, kv in nxt])
        cur = nxt


def _retention(ret_r, kt_r, ret_m, gn, *, batch):
    n_r, cols = ret_r.shape
    seq = n_r // batch
    width = cols // 3
    x_r = ret_r.reshape(batch, seq, cols)
    x_m = ret_m.reshape(batch, N_META, 4 * width)

    def part(rows, j):
        return pl.BlockSpec((None, rows, width), lambda b: (b, 0, j))

    return pl.pallas_call(
        _ret_kernel,
        grid=(batch,),
        in_specs=[part(seq, 0), pl.BlockSpec((None, width, seq), lambda b: (b, 0, 0)), part(seq, 1), part(seq, 2)]
        + [part(N_META, j) for j in range(4)] + [_const_spec(gn.shape)],
        out_specs=[part(seq, 0), part(N_META, 0)],
        out_shape=[jax.ShapeDtypeStruct((batch, seq, width), BF16),
                   jax.ShapeDtypeStruct((batch, N_META, width), BF16)],
        scratch_shapes=[pltpu.VMEM((RET_HEADS, RET_HEAD_DIM, RET_HEAD_DIM), F32),
                        pltpu.VMEM((RET_HEADS, 4, RET_CHUNK, RET_HEAD_DIM), F32)],
        compiler_params=_params(1),
        name="retention",
    )(x_r, kt_r, x_r, x_r, x_m, x_m, x_m, x_m, gn)


def _outffn_body(h_ref, fox_ref, ret_ref, wo_ref, g2_ref, wg_ref, wu_ref, wd_ref, gf_ref, o_ref,
                 hn_scr, act_scr, *, final):
    rows = h_ref.shape[0]
    fox_w = fox_ref.shape[1]
    halves = [slice(0, rows // 2), slice(rows // 2, rows)]
    for r in halves:
        h1 = h_ref[r, :] + _dot(fox_ref[r, :], wo_ref[0:fox_w, :]) + _dot(ret_ref[r, :], wo_ref[fox_w:, :])
        hn_scr[r, :] = _rmsnorm(h1, g2_ref[...]).astype(BF16)
        o_ref[r, :] = h1
    for c0 in range(0, wg_ref.shape[1], FF_CHUNK):
        for r in halves:
            hn = hn_scr[r, :]
            gate = _dot(hn, wg_ref[:, c0:c0 + FF_CHUNK])
            up = _dot(hn, wu_ref[:, c0:c0 + FF_CHUNK])
            act_scr[r, c0:c0 + FF_CHUNK] = (gate * (1.0 / (1.0 + jnp.exp(-gate))) * up).astype(BF16)
    for r in halves:
        out = o_ref[r, :] + _dot(act_scr[r, :], wd_ref[...])
        if final:
            out = _rmsnorm(out, gf_ref[...])
        o_ref[r, :] = out


def _outffn_kernel(*refs, n_tiles, final):
    if final:
        (hr, foxr, retr, wo, g2, wg, wu, wd, gf, outr, hn_scr, act_scr) = refs
    else:
        (hr, foxr, retr, hm, foxm, retm, wo, g2, wg, wu, wd, gf, outr, outm, hn_scr, act_scr) = refs
    body = functools.partial(_outffn_body, final=final)
    t = pl.program_id(0)

    @pl.when(t < n_tiles)
    def _():
        body(hr, foxr, retr, wo, g2, wg, wu, wd, gf, outr, hn_scr, act_scr)

    if not final:
        @pl.when(t == n_tiles)
        def _():
            body(hm, foxm, retm, wo, g2, wg, wu, wd, gf, outm, hn_scr, act_scr)


def _outffn(h_r, fox_r, ret_r, h_m, fox_m, ret_m, wo, g2, wg, wu, wd, gf, *, layer, final):
    n_r, d = h_r.shape
    n_m = h_m.shape[0]
    d_ff = wg.shape[2]
    tm = _row_tile(n_r, OUTFFN_ROWS)
    n_tiles = n_r // tm
    assert n_m <= tm

    def real(cols):
        return pl.BlockSpec((tm, cols), lambda t: (jnp.minimum(t, n_tiles - 1), 0))

    def meta(cols):
        return pl.BlockSpec((n_m, cols), lambda t: (0, 0))

    weights = [_layer_spec(wo.shape, layer), _const_spec(g2.shape), _layer_spec(wg.shape, layer),
               _layer_spec(wu.shape, layer), _layer_spec(wd.shape, layer), _const_spec(gf.shape)]
    real_in = [real(d), real(fox_r.shape[1]), real(ret_r.shape[1])]
    kern = functools.partial(_outffn_kernel, n_tiles=n_tiles, final=final)
    scratch = [pltpu.VMEM((tm, d), BF16),
               pltpu.VMEM((tm, d_ff), BF16)]
    if final:
        return pl.pallas_call(
            kern, grid=(n_tiles,), in_specs=real_in + weights, out_specs=real(d),
            out_shape=jax.ShapeDtypeStruct((n_r, d), F32), scratch_shapes=scratch,
            compiler_params=_params(1), name="outffn_last",
        )(h_r, fox_r, ret_r, wo, g2, wg, wu, wd, gf), None
    meta_in = [meta(d), meta(fox_m.shape[1]), meta(ret_m.shape[1])]
    return pl.pallas_call(
        kern, grid=(n_tiles + 1,), in_specs=real_in + meta_in + weights,
        out_specs=[real(d), meta(d)],
        out_shape=[jax.ShapeDtypeStruct((n_r, d), F32), jax.ShapeDtypeStruct((n_m, d), F32)],
        scratch_shapes=scratch, compiler_params=_params(1), name="outffn",
    )(h_r, fox_r, ret_r, h_m, fox_m, ret_m, wo, g2, wg, wu, wd, gf)


def kernel(x, meta_tokens, attn_norm, w_in, b_fgate, ret_gn, w_out, ffn_norm,
           w_gate, w_up, w_down, final_norm):
    batch, seq, d = x.shape
    depth = w_in.shape[0]
    fox_heads = b_fgate.shape[1]
    fox_w = fox_heads * FOX_HEAD_DIM
    ret_w = RET_HEADS * RET_HEAD_DIM
    d_ff = w_gate.shape[2]
    assert w_in.shape[2] == 3 * fox_w + fox_heads + 4 * ret_w
    assert fox_w % LANES == 0 and fox_heads <= LANES and d_ff % FF_CHUNK == 0
    assert seq % FOX_TILE == 0 and seq % RET_CHUNK == 0

    h_r = x.reshape(batch * seq, d)
    h_m = jnp.broadcast_to(meta_tokens[None].astype(x.dtype), (batch, N_META, d)).reshape(batch * N_META, d)

    inv_freq = ROPE_BASE ** (-jnp.arange(0, RET_HEAD_DIM, 2, dtype=F32) / RET_HEAD_DIM)
    ang = jnp.arange(seq + N_META, dtype=F32)[:, None] * inv_freq[None, :]
    cos, sin = jnp.cos(ang), jnp.sin(ang)
    cs = jnp.concatenate([cos, cos], axis=1)
    sn = jnp.concatenate([-sin, sin], axis=1)
    cs_r, sn_r = cs[N_META:], sn[N_META:]
    cs_m, sn_m = jnp.tile(cs[:N_META], (batch, 1)), jnp.tile(sn[:N_META], (batch, 1))

    split = 3 * fox_w
    w_all = jnp.concatenate(
        [w_in[..., :split], w_in[..., split + fox_heads:], w_in[..., split:split + fox_heads],
         jnp.zeros((depth, d, LANES - fox_heads), w_in.dtype)], axis=-1).astype(BF16)
    wo, wg, wu, wd = (w.astype(BF16) for w in (w_out, w_gate, w_up, w_down))
    sel = jnp.asarray(_bias_selector(fox_heads), BF16)
    gf = final_norm.reshape(1, d)

    for i in range(depth):
        bl = jnp.pad(b_fgate[i], (0, LANES - fox_heads)).reshape(1, LANES)
        fox_r, ret_r, logf_r, kt_r, fox_m, ret_m, logf_m = _inproj(
            h_r, h_m, cs_r, sn_r, cs_m, sn_m, attn_norm[i].reshape(1, d), w_all, bl,
            layer=i, seq=seq, fox_w=fox_w, ret_w=ret_w)
        qaug, kaug, qaug_m, kaug_m = _decay(logf_r, logf_m, sel, batch=batch, n_heads=fox_heads)
        fo_r, fo_m = _fox(fox_r, fox_m, qaug, kaug, qaug_m, kaug_m, batch=batch)
        ro_r, ro_m = _retention(ret_r, kt_r, ret_m, ret_gn[i].reshape(1, ret_w), batch=batch)
        h_r, h_m = _outffn(h_r, fo_r.reshape(batch * seq, fox_w), ro_r.reshape(batch * seq, ret_w),
                           h_m, fo_m.reshape(batch * N_META, fox_w), ro_m.reshape(batch * N_META, ret_w),
                           wo, ffn_norm[i].reshape(1, d), wg, wu, wd, gf,
                           layer=i, final=(i == depth - 1))
    return h_r.reshape(batch, seq, d)
```
